```python
import jax, jax.numpy as jnp
from jax import lax
import numpy as np

D_MODEL = 1024
BATCH = 16
SEQ = 2048
DEPTH = 1

PLE_DIM = 256
EPS = 1e-6
GM_WIDTH = 512
GM_GROUPS = 8
GM_CHUNK = 128
HG_HEADS = 4
HG_KEY = 128
HG_VAL = 128
HG_CHUNK = 64
HG_KW = HG_HEADS * HG_KEY
HG_VW = HG_HEADS * HG_VAL
IN_SPLITS = [GM_WIDTH, GM_WIDTH, HG_KW, HG_KW, HG_VW, HG_VW, D_MODEL, D_MODEL]
IN_COLS = sum(IN_SPLITS)
N_GROUPS = 4
EXP_PER_GROUP = 8
N_EXPERTS = N_GROUPS * EXP_PER_GROUP
TOP_K = 2
D_EXPERT = 512
MOE_BLOCK = 128

kernel_name = "hybrid_gmlp_hgrn2_hmoe_block"


def rmsnorm(x, g):
    xf = x.astype(jnp.float32)
    y = xf * lax.rsqrt(jnp.mean(xf * xf, axis=-1, keepdims=True) + EPS)
    return y.astype(x.dtype) * g


def layernorm(x, g, b):
    xf = x.astype(jnp.float32)
    mu = jnp.mean(xf, axis=-1, keepdims=True)
    var = jnp.mean(jnp.square(xf - mu), axis=-1, keepdims=True)
    y = (xf - mu) * lax.rsqrt(var + EPS)
    return y.astype(x.dtype) * g + b


def gmlp_branch(u, v, ln_g, ln_b, w_sp, b_sp):
    B, S, _ = u.shape
    n_c = S // GM_CHUNK
    dg = GM_WIDTH // GM_GROUPS
    v = layernorm(v, ln_g, ln_b).reshape(B, n_c, GM_CHUNK, GM_GROUPS, dg)
    mask = jnp.tril(jnp.ones((GM_CHUNK, GM_CHUNK), dtype=bool))
    w = jnp.where(mask[None], w_sp, jnp.zeros_like(w_sp))
    s = jnp.einsum('gts,bcsgd->bctgd', w, v) + b_sp.T[:, :, None]
    return u * s.reshape(B, S, GM_WIDTH)


def hgrn2_branch(q, f_raw, i_in, og, lb, norm_g):
    B, S, _ = q.shape
    C = HG_CHUNK
    n_c = S // C
    dt = q.dtype
    qf = jax.nn.silu(q.astype(jnp.float32)).reshape(B, n_c, C, HG_HEADS, HG_KEY)
    f = lb + (1.0 - lb) * jax.nn.sigmoid(f_raw.astype(jnp.float32))
    log_f = jnp.log(f).reshape(B, n_c, C, HG_HEADS, HG_KEY)
    k = (1.0 - f).reshape(B, n_c, C, HG_HEADS, HG_KEY)
    v = i_in.astype(jnp.float32).reshape(B, n_c, C, HG_HEADS, HG_VAL)
    b = jnp.cumsum(log_f, axis=2)
    b_last = b[:, :, -1:]
    q_dec = qf * jnp.exp(b)
    k_inv = k * jnp.exp(-b)
    k_end = k * jnp.exp(b_last - b)
    mask = jnp.tril(jnp.ones((C, C), dtype=bool))
    att = jnp.einsum('bcthk,bcshk->bchts', q_dec, k_inv)
    att = jnp.where(mask, att, jnp.zeros_like(att))
    o_intra = jnp.einsum('bchts,bcshv->bcthv', att, v)
    dS = jnp.einsum('bcshk,bcshv->bchkv', k_end, v)
    decay = jnp.exp(b_last[:, :, 0])

    def step(s_prev, inp):
        dec, ds = inp
        return dec[..., None] * s_prev + ds, s_prev

    s0 = jnp.zeros((B, HG_HEADS, HG_KEY, HG_VAL), jnp.float32)
    _, s_starts = lax.scan(step, s0, (jnp.moveaxis(decay, 1, 0), jnp.moveaxis(dS, 1, 0)))
    o_inter = jnp.einsum('bcthk,cbhkv->bcthv', q_dec, s_starts)
    o = (o_intra + o_inter).reshape(B, S, HG_HEADS, HG_VAL)
    o = o * lax.rsqrt(jnp.mean(o * o, axis=-1, keepdims=True) + EPS)
    o = o * norm_g.astype(jnp.float32).reshape(HG_HEADS, HG_VAL)
    return o.reshape(B, S, HG_VW).astype(dt) * jax.nn.sigmoid(og)


def hier_moe(h, w_grp, w_exp, w1, w3, w2):
    B, S, D = h.shape
    T = B * S
    xt = h.reshape(T, D)
    grp_logits = (xt @ w_grp).astype(jnp.float32)
    grp_prob = jax.nn.softmax(grp_logits, axis=-1)
    p_g, g_sel = lax.top_k(grp_prob, 1)
    exp_logits = (xt @ w_exp).astype(jnp.float32).reshape(T, N_GROUPS, EXP_PER_GROUP)
    sel_logits = jnp.take_along_axis(exp_logits, g_sel[:, :, None], axis=1)[:, 0]
    top_l, top_e = lax.top_k(sel_logits, TOP_K)
    gate = jax.nn.softmax(top_l, axis=-1) * p_g

    A = T * TOP_K
    eid = (g_sel * EXP_PER_GROUP + top_e).reshape(A)
    tok = jnp.repeat(jnp.arange(T, dtype=jnp.int32), TOP_K)
    wa = gate.reshape(A)
    order = jnp.argsort(eid)
    eid_s, tok_s, w_s = eid[order], tok[order], wa[order]
    counts = jnp.bincount(eid, length=N_EXPERTS)
    padded = ((counts + MOE_BLOCK - 1) // MOE_BLOCK) * MOE_BLOCK
    pad_end = jnp.cumsum(padded)
    pad_start = pad_end - padded
    start = jnp.cumsum(counts) - counts
    rank = jnp.arange(A, dtype=jnp.int32) - start[eid_s]
    dest = pad_start[eid_s] + rank
    n_blk = -(-A // MOE_BLOCK) + N_EXPERTS
    P = n_blk * MOE_BLOCK
    buf = jnp.zeros((P, D), xt.dtype).at[dest].set(xt[tok_s])
    blk_start = jnp.arange(n_blk, dtype=jnp.int32) * MOE_BLOCK
    blk_e = jnp.minimum(jnp.searchsorted(pad_end, blk_start, side='right'), N_EXPERTS - 1)

    def expert_block(args):
        xb, e = args
        return (jax.nn.silu(xb @ w1[e]) * (xb @ w3[e])) @ w2[e]

    ybuf = lax.map(expert_block, (buf.reshape(n_blk, MOE_BLOCK, D), blk_e)).reshape(P, D)
    contrib = ybuf[dest] * w_s[:, None].astype(xt.dtype)
    out = jax.ops.segment_sum(contrib, tok_s, num_segments=T)
    return out.reshape(B, S, D)


def setup_inputs(seed: int = 0) -> dict:
    key = jax.random.key(seed)
    ks = jax.random.split(key, 24)

    def nrm(k, shape, scale):
        return jax.random.normal(k, shape, jnp.float32) * scale

    def gain(k, shape):
        return 1.0 + 0.05 * jax.random.normal(k, shape, jnp.float32)

    L = DEPTH
    return {
        "x": nrm(ks[0], (BATCH, SEQ, D_MODEL), 1.0),
        "p": nrm(ks[1], (DEPTH, BATCH, SEQ, PLE_DIM), 1.0),
        "g_mix": gain(ks[2], (L, D_MODEL)),
        "w_in": nrm(ks[3], (L, D_MODEL, IN_COLS), D_MODEL ** -0.5),
        "gm_ln_g": gain(ks[4], (L, GM_WIDTH)),
        "gm_ln_b": nrm(ks[5], (L, GM_WIDTH), 0.02),
        "gm_w_sp": nrm(ks[6], (L, GM_GROUPS, GM_CHUNK, GM_CHUNK), 0.1 * GM_CHUNK ** -0.5),
        "gm_b_sp": gain(ks[7], (L, GM_GROUPS, GM_CHUNK)),
        "w_up_a": nrm(ks[8], (L, GM_WIDTH, D_MODEL), GM_WIDTH ** -0.5),
        "hg_lb_param": gain(ks[9], (L + 1, HG_KW)),
        "hg_norm_g": gain(ks[10], (L, HG_VW)),
        "w_up_b": nrm(ks[11], (L, HG_VW, D_MODEL), HG_VW ** -0.5),
        "w_out": nrm(ks[12], (L, D_MODEL, D_MODEL), D_MODEL ** -0.5),
        "g_ffn": gain(ks[13], (L, D_MODEL)),
        "w_grp": nrm(ks[14], (L, D_MODEL, N_GROUPS), D_MODEL ** -0.5),
        "w_exp": nrm(ks[15], (L, D_MODEL, N_EXPERTS), D_MODEL ** -0.5),
        "w1": nrm(ks[16], (L, N_EXPERTS, D_MODEL, D_EXPERT), D_MODEL ** -0.5),
        "w3": nrm(ks[17], (L, N_EXPERTS, D_MODEL, D_EXPERT), D_MODEL ** -0.5),
        "w2": nrm(ks[18], (L, N_EXPERTS, D_EXPERT, D_MODEL), D_EXPERT ** -0.5),
        "g_ple": gain(ks[19], (L, D_MODEL)),
        "w_pg": nrm(ks[20], (L, D_MODEL, D_MODEL), D_MODEL ** -0.5),
        "w_ple": nrm(ks[21], (L, PLE_DIM, D_MODEL), PLE_DIM ** -0.5),
        "g_final": gain(ks[22], (D_MODEL,)),
    }


def reference(x, p, g_mix, w_in, gm_ln_g, gm_ln_b, gm_w_sp, gm_b_sp, w_up_a,
              hg_lb_param, hg_norm_g, w_up_b, w_out, g_ffn, w_grp, w_exp,
              w1, w3, w2, g_ple, w_pg, w_ple, g_final):
    lb_all = jnp.cumsum(jax.nn.softmax(hg_lb_param.astype(jnp.float32), axis=0), axis=0)
    split_idx = [int(s) for s in np.cumsum(IN_SPLITS)[:-1]]
    for i in range(DEPTH):
        h = rmsnorm(x, g_mix[i])
        z = h @ w_in[i]
        u, v, hq, hf, hi, hog, gate_a, gate_b = jnp.split(z, split_idx, axis=-1)
        y_a = gmlp_branch(jax.nn.gelu(u), jax.nn.gelu(v), gm_ln_g[i], gm_ln_b[i], gm_w_sp[i], gm_b_sp[i])
        y_b = hgrn2_branch(hq, hf, hi, hog, lb_all[i], hg_norm_g[i])
        merged = jax.nn.sigmoid(gate_a) * (y_a @ w_up_a[i]) + jax.nn.sigmoid(gate_b) * (y_b @ w_up_b[i])
        x = x + merged @ w_out[i]
        h = rmsnorm(x, g_ffn[i])
        x = x + hier_moe(h, w_grp[i], w_exp[i], w1[i], w3[i], w2[i])
        h = rmsnorm(x, g_ple[i])
        x = x + jax.nn.sigmoid(h @ w_pg[i]) * (p[i] @ w_ple[i])
    return rmsnorm(x, g_final)
```

```python
import functools

import jax
import jax.numpy as jnp
from jax import lax
from jax.experimental import pallas as pl
from jax.experimental.pallas import tpu as pltpu

F32 = jnp.float32
BF16 = jnp.bfloat16

EPS = 1e-6
GM_WIDTH = 512
GM_GROUPS = 8
GM_CHUNK = 128
HG_HEADS = 4
HG_KEY = 128
HG_CHUNK = 64
HG_KW = HG_HEADS * HG_KEY
N_GROUPS = 4
EXP_PER_GROUP = 8
N_EXPERTS = N_GROUPS * EXP_PER_GROUP
D_EXPERT = 512
LANES = 128
ROUTE_COLS = LANES
EXP_ROW0 = 8

MIX_TILE = 512
ROUTE_TILE = 512
FFN_BLOCK = 256
VMEM_LIMIT = 56 * 1024 * 1024


def _rms(x, g):
    return x * lax.rsqrt(jnp.mean(x * x, axis=-1, keepdims=True) + EPS) * g


def _dot(a, b):
    return jnp.dot(a, b, preferred_element_type=F32)


def _dot_nt(a, b):
    return lax.dot_general(a, b, (((1,), (1,)), ((), ())), preferred_element_type=F32)


def _dot_tn(a, b):
    return lax.dot_general(a, b, (((0,), (0,)), ((), ())), preferred_element_type=F32)


def _mixer_kernel(x_ref, gmix_ref, win_ref, lng_ref, lnb_ref, wsp_ref, bfull_ref, wupa_ref,
                  lbp_ref, ng_ref, wupb_ref, wout_ref, gffn_ref, wrt2_ref, wrt1_ref,
                  x1_ref, h2_ref, lg_ref,
                  z_ref, h_ref, ya_ref, yb_ref, m_ref, st_ref):
    ts = x_ref.shape[1]
    in_cols = win_ref.shape[1]

    @pl.when(pl.program_id(1) == 0)
    def _():
        st_ref[...] = jnp.zeros_like(st_ref)

    h_ref[...] = _rms(x_ref[0], gmix_ref[...]).astype(BF16)
    for n in range(0, in_cols, 512):
        z_ref[:, n:n + 512] = _dot(h_ref[...], win_ref[:, n:n + 512])

    row = lax.broadcasted_iota(jnp.int32, (GM_CHUNK, GM_CHUNK), 0)
    col = lax.broadcasted_iota(jnp.int32, (GM_CHUNK, GM_CHUNK), 1)
    tril = row >= col
    low_half = col < (LANES // 2)

    def gm_body(c, carry):
        r0 = pl.multiple_of(c * GM_CHUNK, GM_CHUNK)
        u = jax.nn.gelu(z_ref[pl.ds(r0, GM_CHUNK), 0:GM_WIDTH])
        v = jax.nn.gelu(z_ref[pl.ds(r0, GM_CHUNK), GM_WIDTH:2 * GM_WIDTH])
        d = v - jnp.mean(v, axis=-1, keepdims=True)
        var = jnp.mean(d * d, axis=-1, keepdims=True)
        vn = d * lax.rsqrt(var + EPS) * lng_ref[...] + lnb_ref[...]
        parts = []
        for p in range(GM_GROUPS // 2):
            vp = vn[:, p * LANES:(p + 1) * LANES]
            v_lo = jnp.where(low_half, vp, 0.0).astype(BF16)
            v_hi = jnp.where(low_half, 0.0, vp).astype(BF16)
            w_lo = jnp.where(tril, wsp_ref[2 * p], 0.0).astype(BF16)
            w_hi = jnp.where(tril, wsp_ref[2 * p + 1], 0.0).astype(BF16)
            parts.append(_dot(w_lo, v_lo) + _dot(w_hi, v_hi))
        s = jnp.concatenate(parts, axis=1) + bfull_ref[...]
        ya_ref[pl.ds(r0, GM_CHUNK), :] = (u * s).astype(BF16)
        return carry

    lax.fori_loop(0, ts // GM_CHUNK, gm_body, 0)

    lbp = lbp_ref[...]
    lmax = jnp.maximum(lbp[0:1], lbp[1:2])
    e0 = jnp.exp(lbp[0:1] - lmax)
    e1 = jnp.exp(lbp[1:2] - lmax)
    lb = e0 / (e0 + e1)

    crow = lax.broadcasted_iota(jnp.int32, (HG_CHUNK, HG_CHUNK), 0)
    ccol = lax.broadcasted_iota(jnp.int32, (HG_CHUNK, HG_CHUNK), 1)
    ctril = crow >= ccol
    tri = jnp.where(ctril, 1.0, 0.0).astype(BF16)
    q0 = 2 * GM_WIDTH

    def hg_body(c, carry):
        r0 = pl.multiple_of(c * HG_CHUNK, HG_CHUNK)
        zq = z_ref[pl.ds(r0, HG_CHUNK), q0:q0 + HG_KW]
        zf = z_ref[pl.ds(r0, HG_CHUNK), q0 + HG_KW:q0 + 2 * HG_KW]
        vi = z_ref[pl.ds(r0, HG_CHUNK), q0 + 2 * HG_KW:q0 + 3 * HG_KW]
        zog = z_ref[pl.ds(r0, HG_CHUNK), q0 + 3 * HG_KW:q0 + 4 * HG_KW]
        qf = zq * jax.nn.sigmoid(zq)
        f = lb + (1.0 - lb) * jax.nn.sigmoid(zf)
        logf = jnp.log(f)
        k = 1.0 - f
        lhi = logf.astype(BF16)
        llo = (logf - lhi.astype(F32)).astype(BF16)
        b = _dot(tri, lhi) + _dot(tri, llo)
        b_last = b[HG_CHUNK - 1:HG_CHUNK, :]
        q_dec = (qf * jnp.exp(b)).astype(BF16)
        k_inv = (k * jnp.exp(-b)).astype(BF16)
        k_end = (k * jnp.exp(b_last - b)).astype(BF16)
        decay = jnp.exp(b_last)
        vb = vi.astype(BF16)
        og = jax.nn.sigmoid(zog)
        outs = []
        for hh in range(HG_HEADS):
            sl = slice(hh * HG_KEY, (hh + 1) * HG_KEY)
            att = jnp.where(ctril, _dot_nt(q_dec[:, sl], k_inv[:, sl]), 0.0)
            st = st_ref[hh]
            o = _dot(att.astype(BF16), vb[:, sl]) + _dot_nt(q_dec[:, sl], st.astype(BF16))
            st_ref[hh] = st * decay[:, sl] + _dot_tn(vb[:, sl], k_end[:, sl])
            o = o * lax.rsqrt(jnp.mean(o * o, axis=-1, keepdims=True) + EPS) * ng_ref[:, sl]
            outs.append(o * og[:, sl])
        yb_ref[pl.ds(r0, HG_CHUNK), :] = jnp.concatenate(outs, axis=1).astype(BF16)
        return carry

    lax.fori_loop(0, ts // HG_CHUNK, hg_body, 0)

    ga0 = q0 + 4 * HG_KW
    gb0 = ga0 + x_ref.shape[2]
    for n in range(0, x_ref.shape[2], 512):
        ua = _dot(ya_ref[...], wupa_ref[:, n:n + 512])
        ub = _dot(yb_ref[...], wupb_ref[:, n:n + 512])
        merged = (jax.nn.sigmoid(z_ref[:, ga0 + n:ga0 + n + 512]) * ua
                  + jax.nn.sigmoid(z_ref[:, gb0 + n:gb0 + n + 512]) * ub)
        m_ref[:, n:n + 512] = merged.astype(BF16)
    x1 = x_ref[0] + _dot(m_ref[...], wout_ref[...])
    x1_ref[0] = x1
    h2 = _rms(x1, gffn_ref[...])
    h2_ref[0] = h2
    hi = h2.astype(BF16)
    lo = (h2 - hi.astype(F32)).astype(BF16)
    l2 = _dot(hi, wrt2_ref[...])
    lg_ref[0] = l2[:, :ROUTE_COLS] + l2[:, ROUTE_COLS:] + _dot(lo, wrt1_ref[...])


def _mixer(x, g_mix, w_in, ln_g, ln_b, w_sp, bfull, w_up_a, lbp, norm_g, w_up_b, w_out, g_ffn,
           w_rt2, w_rt1):
    bsz, seq, d = x.shape
    ts = min(MIX_TILE, seq)
    in_cols = w_in.shape[1]

    def full(a):
        nd = a.ndim
        return pl.BlockSpec(a.shape, lambda b, j, _nd=nd: (0,) * _nd, pipeline_mode=pl.Buffered(1))

    consts = (g_mix, w_in, ln_g, ln_b, w_sp, bfull, w_up_a, lbp, norm_g, w_up_b, w_out, g_ffn,
              w_rt2, w_rt1)
    tile = lambda w: pl.BlockSpec((1, ts, w), lambda b, j: (b, j, 0))
    return pl.pallas_call(
        _mixer_kernel,
        grid=(bsz, seq // ts),
        in_specs=[tile(d)] + [full(a) for a in consts],
        out_specs=[tile(d), tile(d), tile(ROUTE_COLS)],
        out_shape=[jax.ShapeDtypeStruct((bsz, seq, d), F32),
                   jax.ShapeDtypeStruct((bsz, seq, d), F32),
                   jax.ShapeDtypeStruct((bsz, seq, ROUTE_COLS), F32)],
        scratch_shapes=[pltpu.VMEM((ts, in_cols), F32),
                        pltpu.VMEM((ts, d), BF16),
                        pltpu.VMEM((ts, GM_WIDTH), BF16),
                        pltpu.VMEM((ts, HG_KW), BF16),
                        pltpu.VMEM((ts, d), BF16),
                        pltpu.VMEM((HG_HEADS, HG_KEY, HG_KEY), F32)],
        compiler_params=pltpu.CompilerParams(
            dimension_semantics=("arbitrary", "arbitrary"), vmem_limit_bytes=VMEM_LIMIT),
        name="mixer",
    )(x, *consts)


def _route_kernel(lg_ref, utri_ref, dest_ref, gt_ref, binfo_ref, rec_ref, cnt_ref):
    phase = pl.program_id(0)
    i = pl.program_id(1)
    r = lg_ref.shape[0]
    sub8 = lax.broadcasted_iota(jnp.int32, (8, r), 0)
    e32 = lax.broadcasted_iota(jnp.int32, (N_EXPERTS, r), 0).astype(F32)

    @pl.when(jnp.logical_and(phase == 0, i == 0))
    def _():
        cnt_ref[...] = jnp.zeros_like(cnt_ref)

    @pl.when(phase == 0)
    def _():
        lt = lg_ref[...].T
        l0, l1, l2, l3 = lt[0:1], lt[1:2], lt[2:3], lt[3:4]
        gmax = jnp.maximum(jnp.maximum(l0, l1), jnp.maximum(l2, l3))
        gsum = (jnp.exp(l0 - gmax) + jnp.exp(l1 - gmax)) + (jnp.exp(l2 - gmax) + jnp.exp(l3 - gmax))
        p_g = 1.0 / gsum
        gsel = jnp.where(l0 == gmax, 0, jnp.where(l1 == gmax, 1, jnp.where(l2 == gmax, 2, 3)))
        eg = [lt[EXP_ROW0 + EXP_PER_GROUP * g:EXP_ROW0 + EXP_PER_GROUP * (g + 1)]
              for g in range(N_GROUPS)]
        sel = jnp.where(gsel == 0, eg[0], jnp.where(gsel == 1, eg[1],
                                                     jnp.where(gsel == 2, eg[2], eg[3])))
        m1 = jnp.max(sel, axis=0, keepdims=True)
        i1 = jnp.min(jnp.where(sel == m1, sub8, EXP_PER_GROUP), axis=0, keepdims=True)
        sel2 = jnp.where(sub8 == i1, -jnp.inf, sel)
        m2 = jnp.max(sel2, axis=0, keepdims=True)
        i2 = jnp.min(jnp.where(sel2 == m2, sub8, EXP_PER_GROUP), axis=0, keepdims=True)
        ex = jnp.exp(m2 - m1)
        g1 = 1.0 / (1.0 + ex)
        g2 = ex * g1
        eid1 = (gsel * EXP_PER_GROUP + i1).astype(F32)
        eid2 = (gsel * EXP_PER_GROUP + i2).astype(F32)
        oh1 = jnp.where(e32 == eid1, 1.0, 0.0)
        oh2 = jnp.where(e32 == eid2, 1.0, 0.0)
        cum1 = _dot(oh1.astype(BF16), utri_ref[...])
        cum2 = _dot(oh2.astype(BF16), utri_ref[...])
        tot1 = jnp.sum(oh1, axis=1, keepdims=True)
        tot2 = jnp.sum(oh2, axis=1, keepdims=True)
        base = cnt_ref[:, 0:1]
        rank1 = jnp.sum(oh1 * (base + cum1), axis=0, keepdims=True)
        rank2 = jnp.sum(oh2 * (base + tot1 + cum2), axis=0, keepdims=True)
        cnt_ref[...] = cnt_ref[...] + (tot1 + tot2)
        rec = jnp.where(sub8 == 0, eid1, jnp.where(sub8 == 1, eid2, jnp.where(
            sub8 == 2, rank1, jnp.where(sub8 == 3, rank2, jnp.where(
                sub8 == 4, g1 * p_g, jnp.where(sub8 == 5, g2 * p_g, 0.0))))))
        rec_ref[i] = rec

    @pl.when(phase == 1)
    def _():
        cnt = cnt_ref[...]
        padded = jnp.floor((cnt + (FFN_BLOCK - 1)) * (1.0 / FFN_BLOCK)) * FFN_BLOCK
        rows = lax.broadcasted_iota(jnp.int32, cnt.shape, 0)
        pad_end = padded
        for s in (1, 2, 4, 8, 16):
            pad_end = pad_end + jnp.where(rows >= s, pltpu.roll(pad_end, s, axis=0), 0.0)
        pad_start = (pad_end - padded)[:, 0:1]
        rec = rec_ref[i]
        d1 = rec[2:3] + jnp.sum(jnp.where(e32 == rec[0:1], pad_start, 0.0), axis=0, keepdims=True)
        d2 = rec[3:4] + jnp.sum(jnp.where(e32 == rec[1:2], pad_start, 0.0), axis=0, keepdims=True)
        dest_ref[0] = jnp.where(sub8 == 0, d1, jnp.where(sub8 == 1, d2, 0.0)).astype(jnp.int32)
        subw = lax.broadcasted_iota(jnp.int32, (ROUTE_COLS, r), 0)
        gates = jnp.where(subw == 0, rec[4:5], jnp.where(subw == 1, rec[5:6], 0.0))
        gt_ref[...] = gates.T
        nbp = binfo_ref.shape[1]
        blk_start = (lax.broadcasted_iota(jnp.int32, (N_EXPERTS, nbp), 1) * FFN_BLOCK).astype(F32)
        n_le = jnp.sum(jnp.where(pad_end[:, 0:1] <= blk_start, 1.0, 0.0), axis=0, keepdims=True)
        binfo_ref[...] = jnp.broadcast_to(n_le, binfo_ref.shape).astype(jnp.int32)


def _route(logits, nbp):
    t = logits.shape[0]
    r = min(ROUTE_TILE, t)
    nt = t // r
    utri = jnp.triu(jnp.ones((r, r), F32), k=1).astype(BF16)
    return pl.pallas_call(
        _route_kernel,
        grid=(2, nt),
        in_specs=[pl.BlockSpec((r, ROUTE_COLS), lambda p, i: (i * (1 - p) + (nt - 1) * p, 0)),
                  pl.BlockSpec((r, r), lambda p, i: (0, 0))],
        out_specs=[pl.BlockSpec((1, 8, r), lambda p, i: (i * p, 0, 0)),
                   pl.BlockSpec((r, ROUTE_COLS), lambda p, i: (i * p, 0)),
                   pl.BlockSpec((8, nbp), lambda p, i: (0, 0))],
        out_shape=[jax.ShapeDtypeStruct((nt, 8, r), jnp.int32),
                   jax.ShapeDtypeStruct((t, ROUTE_COLS), F32),
                   jax.ShapeDtypeStruct((8, nbp), jnp.int32)],
        scratch_shapes=[pltpu.VMEM((nt, 8, r), F32),
                        pltpu.VMEM((N_EXPERTS, LANES), F32)],
        compiler_params=pltpu.CompilerParams(
            dimension_semantics=("arbitrary", "arbitrary"), vmem_limit_bytes=VMEM_LIMIT),
        name="route",
    )(logits, utri)


def _row_copy(src_hbm, src_row, dst_hbm, dst_row, sem):
    return pltpu.make_async_copy(src_hbm.at[pl.ds(src_row, 1)], dst_hbm.at[pl.ds(dst_row, 1)], sem)


def _scatter_kernel(dest_ref, h2_hbm, buf_in_hbm, buf_hbm, sem):
    del buf_in_hbm
    r = dest_ref.shape[2]
    base = pl.program_id(0) * r

    def issue(t, carry):
        for k in range(2):
            _row_copy(h2_hbm, base + t, buf_hbm, dest_ref[0, k, t], sem).start()
        return carry

    lax.fori_loop(0, r, issue, 0)

    def drain(t, carry):
        _row_copy(h2_hbm, 0, buf_hbm, 0, sem).wait()
        return carry

    lax.fori_loop(0, 2 * r, drain, 0)


def _scatter(dest, h2, n_rows):
    nt, _, r = dest.shape
    buf0 = jnp.zeros((n_rows, h2.shape[1]), F32)
    return pl.pallas_call(
        _scatter_kernel,
        grid=(nt,),
        in_specs=[pl.BlockSpec((1, 8, r), lambda i: (i, 0, 0), memory_space=pltpu.SMEM),
                  pl.BlockSpec(memory_space=pl.ANY),
                  pl.BlockSpec(memory_space=pl.ANY)],
        out_specs=pl.BlockSpec(memory_space=pl.ANY),
        out_shape=jax.ShapeDtypeStruct(buf0.shape, F32),
        scratch_shapes=[pltpu.SemaphoreType.DMA(())],
        input_output_aliases={2: 0},
        compiler_params=pltpu.CompilerParams(
            dimension_semantics=("arbitrary",), has_side_effects=True),
        name="scatter",
    )(dest, h2, buf0)


def _ffn_kernel(be_ref, x_ref, w13_ref, w2_ref, y_ref):
    used = be_ref[pl.program_id(0)] < N_EXPERTS

    @pl.when(used)
    def _():
        h13 = _dot(x_ref[...].astype(BF16), w13_ref[0])
        a = h13[:, :D_EXPERT]
        act = (a * jax.nn.sigmoid(a) * h13[:, D_EXPERT:]).astype(BF16)
        y_ref[...] = _dot(act, w2_ref[0])

    @pl.when(jnp.logical_not(used))
    def _():
        y_ref[...] = jnp.zeros_like(y_ref)


def _ffn(blk_e, buf, w13, w2):
    n_rows, d = buf.shape
    nb = n_rows // FFN_BLOCK
    emap = lambda i, be: (jnp.minimum(be[i], N_EXPERTS - 1), 0, 0)
    return pl.pallas_call(
        _ffn_kernel,
        grid_spec=pltpu.PrefetchScalarGridSpec(
            num_scalar_prefetch=1,
            grid=(nb,),
            in_specs=[pl.BlockSpec((FFN_BLOCK, d), lambda i, be: (i, 0)),
                      pl.BlockSpec((1, d, 2 * D_EXPERT), emap),
                      pl.BlockSpec((1, D_EXPERT, d), emap)],
            out_specs=pl.BlockSpec((FFN_BLOCK, d), lambda i, be: (i, 0))),
        out_shape=jax.ShapeDtypeStruct((n_rows, d), F32),
        compiler_params=pltpu.CompilerParams(
            dimension_semantics=("arbitrary",), vmem_limit_bytes=VMEM_LIMIT),
        name="ffn",
    )(blk_e, buf, w13, w2)


def _final_kernel(dest_ref, x1_ref, p_ref, gt_ref, ybuf_hbm, gple_ref, wpg_ref, wple_ref,
                  gfin_ref, out_ref, y_ref, sem):
    r = x1_ref.shape[0]

    def issue(t, carry):
        for k in range(2):
            pltpu.make_async_copy(ybuf_hbm.at[pl.ds(dest_ref[0, k, t], 1)],
                                  y_ref.at[k, pl.ds(t, 1)], sem).start()
        return carry

    lax.fori_loop(0, r, issue, 0)

    def drain(t, carry):
        pltpu.make_async_copy(ybuf_hbm.at[pl.ds(0, 1)], y_ref.at[0, pl.ds(0, 1)], sem).wait()
        return carry

    lax.fori_loop(0, 2 * r, drain, 0)

    x2 = x1_ref[...] + gt_ref[:, 0:1] * y_ref[0] + gt_ref[:, 1:2] * y_ref[1]
    h3 = _rms(x2, gple_ref[...]).astype(BF16)
    gate = jax.nn.sigmoid(_dot(h3, wpg_ref[...]))
    x3 = x2 + gate * _dot(p_ref[...].astype(BF16), wple_ref[...])
    out_ref[...] = _rms(x3, gfin_ref[...])


def _final(dest, x1, p, gt, ybuf, g_ple, w_pg, w_ple, g_final):
    t, d = x1.shape
    nt, _, r = dest.shape

    def full(a):
        nd = a.ndim
        return pl.BlockSpec(a.shape, lambda i, _nd=nd: (0,) * _nd, pipeline_mode=pl.Buffered(1))

    return pl.pallas_call(
        _final_kernel,
        grid=(nt,),
        in_specs=[pl.BlockSpec((1, 8, r), lambda i: (i, 0, 0), memory_space=pltpu.SMEM),
                  pl.BlockSpec((r, d), lambda i: (i, 0)),
                  pl.BlockSpec((r, p.shape[1]), lambda i: (i, 0)),
                  pl.BlockSpec((r, ROUTE_COLS), lambda i: (i, 0)),
                  pl.BlockSpec(memory_space=pl.ANY),
                  full(g_ple), full(w_pg), full(w_ple), full(g_final)],
        out_specs=pl.BlockSpec((r, d), lambda i: (i, 0)),
        out_shape=jax.ShapeDtypeStruct((t, d), F32),
        scratch_shapes=[pltpu.VMEM((2, r, d), F32), pltpu.SemaphoreType.DMA(())],
        compiler_params=pltpu.CompilerParams(
            dimension_semantics=("arbitrary",), vmem_limit_bytes=VMEM_LIMIT),
        name="final",
    )(dest, x1, p, gt, ybuf, g_ple, w_pg, w_ple, g_final)


def kernel(x, p, g_mix, w_in, gm_ln_g, gm_ln_b, gm_w_sp, gm_b_sp, w_up_a, hg_lb_param, hg_norm_g,
           w_up_b, w_out, g_ffn, w_grp, w_exp, w1, w3, w2, g_ple, w_pg, w_ple, g_final):
    bsz, seq, d = x.shape
    t = bsz * seq
    row = lambda a: a.reshape(1, -1)

    assert w_in.shape[0] == 1 and hg_lb_param.shape[0] == 2, "single-layer block"
    i = 0
    w_rt = jnp.zeros((d, ROUTE_COLS), F32)
    w_rt = w_rt.at[:, 0:N_GROUPS].set(w_grp[i])
    w_rt = w_rt.at[:, EXP_ROW0:EXP_ROW0 + N_EXPERTS].set(w_exp[i])
    w_rt_hi = w_rt.astype(BF16)
    w_rt_lo = (w_rt - w_rt_hi.astype(F32)).astype(BF16)
    w_rt2 = jnp.concatenate([w_rt_hi, w_rt_lo], axis=1)
    bfull = jnp.repeat(gm_b_sp[i].T, GM_WIDTH // GM_GROUPS, axis=1)

    x1, h2, logits = _mixer(
        x, row(g_mix[i]), w_in[i].astype(BF16), row(gm_ln_g[i]), row(gm_ln_b[i]), gm_w_sp[i],
        bfull, w_up_a[i].astype(BF16), hg_lb_param, row(hg_norm_g[i]),
        w_up_b[i].astype(BF16), w_out[i].astype(BF16), row(g_ffn[i]), w_rt2, w_rt_hi)

    n_rows = 2 * t + N_EXPERTS * FFN_BLOCK
    nb = n_rows // FFN_BLOCK
    nbp = -(-nb // LANES) * LANES
    dest, gt, binfo = _route(logits.reshape(t, ROUTE_COLS), nbp)
    buf = _scatter(dest, h2.reshape(t, d), n_rows)
    w13 = jnp.concatenate([w1[i], w3[i]], axis=-1).astype(BF16)
    ybuf = _ffn(binfo[0, :nb], buf, w13, w2[i].astype(BF16))
    out = _final(dest, x1.reshape(t, d), p[i].reshape(t, -1), gt, ybuf, row(g_ple[i]),
                 w_pg[i].astype(BF16), w_ple[i].astype(BF16), row(g_final))
    return out.reshape(bsz, seq, d)
```

```python
import functools

import jax
import jax.numpy as jnp
from jax import lax
from jax.experimental import pallas as pl
from jax.experimental.pallas import tpu as pltpu

F32 = jnp.float32
BF16 = jnp.bfloat16

EPS = 1e-6
GM_WIDTH = 512
GM_GROUPS = 8
GM_CHUNK = 128
HG_HEADS = 4
HG_KEY = 128
HG_CHUNK = 64
HG_KW = HG_HEADS * HG_KEY
N_GROUPS = 4
EXP_PER_GROUP = 8
N_EXPERTS = N_GROUPS * EXP_PER_GROUP
D_EXPERT = 512
LANES = 128
ROUTE_COLS = LANES
EXP_ROW0 = 8

MIX_TILE = 512
ROUTE_TILE = 512
FFN_BLOCK = 256
VMEM_LIMIT = 56 * 1024 * 1024


def _rms(x, g):
    return x * lax.rsqrt(jnp.mean(x * x, axis=-1, keepdims=True) + EPS) * g


def _dot(a, b):
    return jnp.dot(a, b, preferred_element_type=F32)


def _dot_nt(a, b):
    return lax.dot_general(a, b, (((1,), (1,)), ((), ())), preferred_element_type=F32)


def _dot_tn(a, b):
    return lax.dot_general(a, b, (((0,), (0,)), ((), ())), preferred_element_type=F32)


ROW_TILE = 8


def _store_token_rows(ref, val):
    n = val.shape[0]
    for c in range(ROW_TILE):
        ref[pl.ds(c, n, stride=ROW_TILE), :] = val[:, c * LANES:(c + 1) * LANES]


def _load_token_rows(ref, n):
    return jnp.concatenate([ref[pl.ds(c, n, stride=ROW_TILE), :] for c in range(ROW_TILE)], axis=1)


def _mixer_kernel(x_ref, gmix_ref, win_ref, lng_ref, lnb_ref, wsp_ref, bfull_ref, wupa_ref,
                  lbp_ref, ng_ref, wupb_ref, wout_ref, gffn_ref, wrt2_ref, wrt1_ref,
                  x1_ref, h2_ref, lg_ref,
                  z_ref, h_ref, ya_ref, yb_ref, m_ref, st_ref):
    ts = x_ref.shape[1]
    in_cols = win_ref.shape[1]

    @pl.when(pl.program_id(1) == 0)
    def _():
        st_ref[...] = jnp.zeros_like(st_ref)

    h_ref[...] = _rms(x_ref[0], gmix_ref[...]).astype(BF16)
    for n in range(0, in_cols, 512):
        z_ref[:, n:n + 512] = _dot(h_ref[...], win_ref[:, n:n + 512])

    row = lax.broadcasted_iota(jnp.int32, (GM_CHUNK, GM_CHUNK), 0)
    col = lax.broadcasted_iota(jnp.int32, (GM_CHUNK, GM_CHUNK), 1)
    tril = row >= col
    low_half = col < (LANES // 2)

    def gm_body(c, carry):
        r0 = pl.multiple_of(c * GM_CHUNK, GM_CHUNK)
        u = jax.nn.gelu(z_ref[pl.ds(r0, GM_CHUNK), 0:GM_WIDTH])
        v = jax.nn.gelu(z_ref[pl.ds(r0, GM_CHUNK), GM_WIDTH:2 * GM_WIDTH])
        d = v - jnp.mean(v, axis=-1, keepdims=True)
        var = jnp.mean(d * d, axis=-1, keepdims=True)
        vn = d * lax.rsqrt(var + EPS) * lng_ref[...] + lnb_ref[...]
        parts = []
        for p in range(GM_GROUPS // 2):
            vp = vn[:, p * LANES:(p + 1) * LANES]
            v_lo = jnp.where(low_half, vp, 0.0).astype(BF16)
            v_hi = jnp.where(low_half, 0.0, vp).astype(BF16)
            w_lo = jnp.where(tril, wsp_ref[2 * p], 0.0).astype(BF16)
            w_hi = jnp.where(tril, wsp_ref[2 * p + 1], 0.0).astype(BF16)
            parts.append(_dot(w_lo, v_lo) + _dot(w_hi, v_hi))
        s = jnp.concatenate(parts, axis=1) + bfull_ref[...]
        ya_ref[pl.ds(r0, GM_CHUNK), :] = (u * s).astype(BF16)
        return carry

    lax.fori_loop(0, ts // GM_CHUNK, gm_body, 0)

    lbp = lbp_ref[...]
    lmax = jnp.maximum(lbp[0:1], lbp[1:2])
    e0 = jnp.exp(lbp[0:1] - lmax)
    e1 = jnp.exp(lbp[1:2] - lmax)
    lb = e0 / (e0 + e1)

    crow = lax.broadcasted_iota(jnp.int32, (HG_CHUNK, HG_CHUNK), 0)
    ccol = lax.broadcasted_iota(jnp.int32, (HG_CHUNK, HG_CHUNK), 1)
    ctril = crow >= ccol
    tri = jnp.where(ctril, 1.0, 0.0).astype(BF16)
    q0 = 2 * GM_WIDTH

    def hg_body(c, carry):
        r0 = pl.multiple_of(c * HG_CHUNK, HG_CHUNK)
        zq = z_ref[pl.ds(r0, HG_CHUNK), q0:q0 + HG_KW]
        zf = z_ref[pl.ds(r0, HG_CHUNK), q0 + HG_KW:q0 + 2 * HG_KW]
        vi = z_ref[pl.ds(r0, HG_CHUNK), q0 + 2 * HG_KW:q0 + 3 * HG_KW]
        zog = z_ref[pl.ds(r0, HG_CHUNK), q0 + 3 * HG_KW:q0 + 4 * HG_KW]
        qf = zq * jax.nn.sigmoid(zq)
        f = lb + (1.0 - lb) * jax.nn.sigmoid(zf)
        logf = jnp.log(f)
        k = 1.0 - f
        lhi = logf.astype(BF16)
        llo = (logf - lhi.astype(F32)).astype(BF16)
        b = _dot(tri, lhi) + _dot(tri, llo)
        b_last = b[HG_CHUNK - 1:HG_CHUNK, :]
        q_dec = (qf * jnp.exp(b)).astype(BF16)
        k_inv = (k * jnp.exp(-b)).astype(BF16)
        k_end = (k * jnp.exp(b_last - b)).astype(BF16)
        decay = jnp.exp(b_last)
        vb = vi.astype(BF16)
        og = jax.nn.sigmoid(zog)
        outs = []
        for hh in range(HG_HEADS):
            sl = slice(hh * HG_KEY, (hh + 1) * HG_KEY)
            att = jnp.where(ctril, _dot_nt(q_dec[:, sl], k_inv[:, sl]), 0.0)
            st = st_ref[hh]
            o = _dot(att.astype(BF16), vb[:, sl]) + _dot_nt(q_dec[:, sl], st.astype(BF16))
            st_ref[hh] = st * decay[:, sl] + _dot_tn(vb[:, sl], k_end[:, sl])
            o = o * lax.rsqrt(jnp.mean(o * o, axis=-1, keepdims=True) + EPS) * ng_ref[:, sl]
            outs.append(o * og[:, sl])
        yb_ref[pl.ds(r0, HG_CHUNK), :] = jnp.concatenate(outs, axis=1).astype(BF16)
        return carry

    lax.fori_loop(0, ts // HG_CHUNK, hg_body, 0)

    ga0 = q0 + 4 * HG_KW
    gb0 = ga0 + x_ref.shape[2]
    for n in range(0, x_ref.shape[2], 512):
        ua = _dot(ya_ref[...], wupa_ref[:, n:n + 512])
        ub = _dot(yb_ref[...], wupb_ref[:, n:n + 512])
        merged = (jax.nn.sigmoid(z_ref[:, ga0 + n:ga0 + n + 512]) * ua
                  + jax.nn.sigmoid(z_ref[:, gb0 + n:gb0 + n + 512]) * ub)
        m_ref[:, n:n + 512] = merged.astype(BF16)
    x1 = x_ref[0] + _dot(m_ref[...], wout_ref[...])
    x1_ref[0] = x1
    h2 = _rms(x1, gffn_ref[...])
    _store_token_rows(h2_ref, h2)
    hi = h2.astype(BF16)
    lo = (h2 - hi.astype(F32)).astype(BF16)
    l2 = _dot(hi, wrt2_ref[...])
    lg_ref[0] = l2[:, :ROUTE_COLS] + l2[:, ROUTE_COLS:] + _dot(lo, wrt1_ref[...])


def _mixer(x, g_mix, w_in, ln_g, ln_b, w_sp, bfull, w_up_a, lbp, norm_g, w_up_b, w_out, g_ffn,
           w_rt2, w_rt1):
    bsz, seq, d = x.shape
    ts = min(MIX_TILE, seq)
    in_cols = w_in.shape[1]

    def full(a):
        nd = a.ndim
        return pl.BlockSpec(a.shape, lambda b, j, _nd=nd: (0,) * _nd, pipeline_mode=pl.Buffered(1))

    consts = (g_mix, w_in, ln_g, ln_b, w_sp, bfull, w_up_a, lbp, norm_g, w_up_b, w_out, g_ffn,
              w_rt2, w_rt1)
    assert d == ROW_TILE * LANES
    nj = seq // ts
    tile = lambda w: pl.BlockSpec((1, ts, w), lambda b, j: (b, j, 0))
    return pl.pallas_call(
        _mixer_kernel,
        grid=(bsz, nj),
        in_specs=[tile(d)] + [full(a) for a in consts],
        out_specs=[tile(d),
                   pl.BlockSpec((ts * ROW_TILE, LANES), lambda b, j: (b * nj + j, 0)),
                   tile(ROUTE_COLS)],
        out_shape=[jax.ShapeDtypeStruct((bsz, seq, d), F32),
                   jax.ShapeDtypeStruct((bsz * seq * ROW_TILE, LANES), F32),
                   jax.ShapeDtypeStruct((bsz, seq, ROUTE_COLS), F32)],
        scratch_shapes=[pltpu.VMEM((ts, in_cols), F32),
                        pltpu.VMEM((ts, d), BF16),
                        pltpu.VMEM((ts, GM_WIDTH), BF16),
                        pltpu.VMEM((ts, HG_KW), BF16),
                        pltpu.VMEM((ts, d), BF16),
                        pltpu.VMEM((HG_HEADS, HG_KEY, HG_KEY), F32)],
        compiler_params=pltpu.CompilerParams(
            dimension_semantics=("arbitrary", "arbitrary"), vmem_limit_bytes=VMEM_LIMIT),
        name="mixer",
    )(x, *consts)


def _route_kernel(lg_ref, utri_ref, dest_ref, gt_ref, binfo_ref, rec_ref, cnt_ref):
    phase = pl.program_id(0)
    i = pl.program_id(1)
    r = lg_ref.shape[0]
    sub8 = lax.broadcasted_iota(jnp.int32, (8, r), 0)
    e32 = lax.broadcasted_iota(jnp.int32, (N_EXPERTS, r), 0).astype(F32)

    @pl.when(jnp.logical_and(phase == 0, i == 0))
    def _():
        cnt_ref[...] = jnp.zeros_like(cnt_ref)

    @pl.when(phase == 0)
    def _():
        lt = lg_ref[...].T
        l0, l1, l2, l3 = lt[0:1], lt[1:2], lt[2:3], lt[3:4]
        gmax = jnp.maximum(jnp.maximum(l0, l1), jnp.maximum(l2, l3))
        gsum = (jnp.exp(l0 - gmax) + jnp.exp(l1 - gmax)) + (jnp.exp(l2 - gmax) + jnp.exp(l3 - gmax))
        p_g = 1.0 / gsum
        gsel = jnp.where(l0 == gmax, 0, jnp.where(l1 == gmax, 1, jnp.where(l2 == gmax, 2, 3)))
        eg = [lt[EXP_ROW0 + EXP_PER_GROUP * g:EXP_ROW0 + EXP_PER_GROUP * (g + 1)]
              for g in range(N_GROUPS)]
        sel = jnp.where(gsel == 0, eg[0], jnp.where(gsel == 1, eg[1],
                                                     jnp.where(gsel == 2, eg[2], eg[3])))
        m1 = jnp.max(sel, axis=0, keepdims=True)
        i1 = jnp.min(jnp.where(sel == m1, sub8, EXP_PER_GROUP), axis=0, keepdims=True)
        sel2 = jnp.where(sub8 == i1, -jnp.inf, sel)
        m2 = jnp.max(sel2, axis=0, keepdims=True)
        i2 = jnp.min(jnp.where(sel2 == m2, sub8, EXP_PER_GROUP), axis=0, keepdims=True)
        ex = jnp.exp(m2 - m1)
        g1 = 1.0 / (1.0 + ex)
        g2 = ex * g1
        eid1 = (gsel * EXP_PER_GROUP + i1).astype(F32)
        eid2 = (gsel * EXP_PER_GROUP + i2).astype(F32)
        oh1 = jnp.where(e32 == eid1, 1.0, 0.0)
        oh2 = jnp.where(e32 == eid2, 1.0, 0.0)
        cum1 = _dot(oh1.astype(BF16), utri_ref[...])
        cum2 = _dot(oh2.astype(BF16), utri_ref[...])
        tot1 = jnp.sum(oh1, axis=1, keepdims=True)
        tot2 = jnp.sum(oh2, axis=1, keepdims=True)
        base = cnt_ref[:, 0:1]
        rank1 = jnp.sum(oh1 * (base + cum1), axis=0, keepdims=True)
        rank2 = jnp.sum(oh2 * (base + tot1 + cum2), axis=0, keepdims=True)
        cnt_ref[...] = cnt_ref[...] + (tot1 + tot2)
        rec = jnp.where(sub8 == 0, eid1, jnp.where(sub8 == 1, eid2, jnp.where(
            sub8 == 2, rank1, jnp.where(sub8 == 3, rank2, jnp.where(
                sub8 == 4, g1 * p_g, jnp.where(sub8 == 5, g2 * p_g, 0.0))))))
        rec_ref[i] = rec

    @pl.when(phase == 1)
    def _():
        cnt = cnt_ref[...]
        padded = jnp.floor((cnt + (FFN_BLOCK - 1)) * (1.0 / FFN_BLOCK)) * FFN_BLOCK
        rows = lax.broadcasted_iota(jnp.int32, cnt.shape, 0)
        pad_end = padded
        for s in (1, 2, 4, 8, 16):
            pad_end = pad_end + jnp.where(rows >= s, pltpu.roll(pad_end, s, axis=0), 0.0)
        pad_start = (pad_end - padded)[:, 0:1]
        rec = rec_ref[i]
        d1 = rec[2:3] + jnp.sum(jnp.where(e32 == rec[0:1], pad_start, 0.0), axis=0, keepdims=True)
        d2 = rec[3:4] + jnp.sum(jnp.where(e32 == rec[1:2], pad_start, 0.0), axis=0, keepdims=True)
        dest_ref[0] = jnp.where(sub8 == 0, d1, jnp.where(sub8 == 1, d2, 0.0)).astype(jnp.int32)
        subw = lax.broadcasted_iota(jnp.int32, (ROUTE_COLS, r), 0)
        gates = jnp.where(subw == 0, rec[4:5], jnp.where(subw == 1, rec[5:6], 0.0))
        gt_ref[...] = gates.T
        nbp = binfo_ref.shape[1]
        blk_start = (lax.broadcasted_iota(jnp.int32, (N_EXPERTS, nbp), 1) * FFN_BLOCK).astype(F32)
        n_le = jnp.sum(jnp.where(pad_end[:, 0:1] <= blk_start, 1.0, 0.0), axis=0, keepdims=True)
        binfo_ref[...] = jnp.broadcast_to(n_le, binfo_ref.shape).astype(jnp.int32)


def _route(logits, nbp):
    t = logits.shape[0]
    r = min(ROUTE_TILE, t)
    nt = t // r
    utri = jnp.triu(jnp.ones((r, r), F32), k=1).astype(BF16)
    return pl.pallas_call(
        _route_kernel,
        grid=(2, nt),
        in_specs=[pl.BlockSpec((r, ROUTE_COLS), lambda p, i: (i * (1 - p) + (nt - 1) * p, 0)),
                  pl.BlockSpec((r, r), lambda p, i: (0, 0))],
        out_specs=[pl.BlockSpec((1, 8, r), lambda p, i: (i * p, 0, 0)),
                   pl.BlockSpec((r, ROUTE_COLS), lambda p, i: (i * p, 0)),
                   pl.BlockSpec((8, nbp), lambda p, i: (0, 0))],
        out_shape=[jax.ShapeDtypeStruct((nt, 8, r), jnp.int32),
                   jax.ShapeDtypeStruct((t, ROUTE_COLS), F32),
                   jax.ShapeDtypeStruct((8, nbp), jnp.int32)],
        scratch_shapes=[pltpu.VMEM((nt, 8, r), F32),
                        pltpu.VMEM((N_EXPERTS, LANES), F32)],
        compiler_params=pltpu.CompilerParams(
            dimension_semantics=("arbitrary", "arbitrary"), vmem_limit_bytes=VMEM_LIMIT),
        name="route",
    )(logits, utri)


DMA_UNROLL = 8


def _token_copy(src, src_tok, dst, dst_tok, sem):
    return pltpu.make_async_copy(
        src.at[pl.ds(pl.multiple_of(src_tok * ROW_TILE, ROW_TILE), ROW_TILE)],
        dst.at[pl.ds(pl.multiple_of(dst_tok * ROW_TILE, ROW_TILE), ROW_TILE)], sem)


def _scatter_kernel(dest_ref, h2_ref, buf_in_hbm, buf_hbm, sem):
    del buf_in_hbm
    r = dest_ref.shape[2]

    def issue(t, carry):
        for k in range(2):
            _token_copy(h2_ref, t, buf_hbm, dest_ref[0, k, t], sem).start()
        return carry

    lax.fori_loop(0, r, issue, 0, unroll=DMA_UNROLL)

    def drain(t, carry):
        _token_copy(h2_ref, 0, buf_hbm, 0, sem).wait()
        return carry

    lax.fori_loop(0, 2 * r, drain, 0, unroll=DMA_UNROLL)


def _scatter(dest, h2, n_rows):
    nt, _, r = dest.shape
    buf0 = jnp.zeros((n_rows * ROW_TILE, LANES), F32)
    return pl.pallas_call(
        _scatter_kernel,
        grid=(nt,),
        in_specs=[pl.BlockSpec((1, 8, r), lambda i: (i, 0, 0), memory_space=pltpu.SMEM),
                  pl.BlockSpec((r * ROW_TILE, LANES), lambda i: (i, 0)),
                  pl.BlockSpec(memory_space=pl.ANY)],
        out_specs=pl.BlockSpec(memory_space=pl.ANY),
        out_shape=jax.ShapeDtypeStruct(buf0.shape, F32),
        scratch_shapes=[pltpu.SemaphoreType.DMA(())],
        input_output_aliases={2: 0},
        compiler_params=pltpu.CompilerParams(
            dimension_semantics=("arbitrary",), has_side_effects=True),
        name="scatter",
    )(dest, h2, buf0)


def _ffn_kernel(be_ref, x_ref, w1_ref, w3_ref, w2_ref, y_ref, w13_s, w2_s):
    i = pl.program_id(0)
    e = be_ref[i]
    used = e < N_EXPERTS
    fresh = jnp.logical_or(i == 0, e != be_ref[jnp.maximum(i - 1, 0)])

    @pl.when(jnp.logical_and(used, fresh))
    def _():
        w13_s[:, :D_EXPERT] = w1_ref[0].astype(BF16)
        w13_s[:, D_EXPERT:] = w3_ref[0].astype(BF16)
        w2_s[...] = w2_ref[0].astype(BF16)

    @pl.when(used)
    def _():
        h13 = _dot(_load_token_rows(x_ref, FFN_BLOCK).astype(BF16), w13_s[...])
        a = h13[:, :D_EXPERT]
        act = (a * jax.nn.sigmoid(a) * h13[:, D_EXPERT:]).astype(BF16)
        _store_token_rows(y_ref, _dot(act, w2_s[...]))

    @pl.when(jnp.logical_not(used))
    def _():
        y_ref[...] = jnp.zeros_like(y_ref)


def _ffn(blk_e, buf, w1, w3, w2):
    nb = buf.shape[0] // (FFN_BLOCK * ROW_TILE)
    d = w1.shape[1]
    emap = lambda i, be: (jnp.minimum(be[i], N_EXPERTS - 1), 0, 0)
    rows = pl.BlockSpec((FFN_BLOCK * ROW_TILE, LANES), lambda i, be: (i, 0))
    return pl.pallas_call(
        _ffn_kernel,
        grid_spec=pltpu.PrefetchScalarGridSpec(
            num_scalar_prefetch=1,
            grid=(nb,),
            in_specs=[rows,
                      pl.BlockSpec((1, d, D_EXPERT), emap),
                      pl.BlockSpec((1, d, D_EXPERT), emap),
                      pl.BlockSpec((1, D_EXPERT, d), emap)],
            out_specs=rows,
            scratch_shapes=[pltpu.VMEM((d, 2 * D_EXPERT), BF16),
                            pltpu.VMEM((D_EXPERT, d), BF16)]),
        out_shape=jax.ShapeDtypeStruct(buf.shape, F32),
        compiler_params=pltpu.CompilerParams(
            dimension_semantics=("arbitrary",), vmem_limit_bytes=VMEM_LIMIT),
        name="ffn",
    )(blk_e, buf, w1, w3, w2)


def _final_kernel(dest_ref, dnext_ref, x1_ref, p_ref, gt_ref, ybuf_hbm, gple_ref, wpg_ref,
                  wple_ref, gfin_ref, out_ref, y_ref, sems):
    i = pl.program_id(0)
    r = x1_ref.shape[0]
    slot = i % 2

    def gather(d_ref, s):
        def issue(t, carry):
            for k in range(2):
                _token_copy(ybuf_hbm, d_ref[0, k, t], y_ref.at[s, k], t, sems.at[s]).start()
            return carry

        lax.fori_loop(0, r, issue, 0, unroll=DMA_UNROLL)

    @pl.when(i == 0)
    def _():
        gather(dest_ref, 0)

    @pl.when(i + 1 < pl.num_programs(0))
    def _():
        gather(dnext_ref, 1 - slot)

    def drain(t, carry):
        _token_copy(ybuf_hbm, 0, y_ref.at[slot, 0], 0, sems.at[slot]).wait()
        return carry

    lax.fori_loop(0, 2 * r, drain, 0, unroll=DMA_UNROLL)

    x2 = (x1_ref[...] + gt_ref[:, 0:1] * _load_token_rows(y_ref.at[slot, 0], r)
          + gt_ref[:, 1:2] * _load_token_rows(y_ref.at[slot, 1], r))
    h3 = _rms(x2, gple_ref[...]).astype(BF16)
    gate = jax.nn.sigmoid(_dot(h3, wpg_ref[...]))
    x3 = x2 + gate * _dot(p_ref[...].astype(BF16), wple_ref[...])
    out_ref[...] = _rms(x3, gfin_ref[...])


def _final(dest, x1, p, gt, ybuf, g_ple, w_pg, w_ple, g_final):
    t, d = x1.shape
    nt, _, r = dest.shape

    def full(a):
        nd = a.ndim
        return pl.BlockSpec(a.shape, lambda i, _nd=nd: (0,) * _nd, pipeline_mode=pl.Buffered(1))

    return pl.pallas_call(
        _final_kernel,
        grid=(nt,),
        in_specs=[pl.BlockSpec((1, 8, r), lambda i: (i, 0, 0), memory_space=pltpu.SMEM),
                  pl.BlockSpec((1, 8, r), lambda i: (jnp.minimum(i + 1, nt - 1), 0, 0),
                               memory_space=pltpu.SMEM),
                  pl.BlockSpec((r, d), lambda i: (i, 0)),
                  pl.BlockSpec((r, p.shape[1]), lambda i: (i, 0)),
                  pl.BlockSpec((r, ROUTE_COLS), lambda i: (i, 0)),
                  pl.BlockSpec(memory_space=pl.ANY),
                  full(g_ple), full(w_pg), full(w_ple), full(g_final)],
        out_specs=pl.BlockSpec((r, d), lambda i: (i, 0)),
        out_shape=jax.ShapeDtypeStruct((t, d), F32),
        scratch_shapes=[pltpu.VMEM((2, 2, r * ROW_TILE, LANES), F32),
                        pltpu.SemaphoreType.DMA((2,))],
        compiler_params=pltpu.CompilerParams(
            dimension_semantics=("arbitrary",), vmem_limit_bytes=VMEM_LIMIT),
        name="final",
    )(dest, dest, x1, p, gt, ybuf, g_ple, w_pg, w_ple, g_final)


def kernel(x, p, g_mix, w_in, gm_ln_g, gm_ln_b, gm_w_sp, gm_b_sp, w_up_a, hg_lb_param, hg_norm_g,
           w_up_b, w_out, g_ffn, w_grp, w_exp, w1, w3, w2, g_ple, w_pg, w_ple, g_final):
    bsz, seq, d = x.shape
    t = bsz * seq
    row = lambda a: a.reshape(1, -1)

    assert w_in.shape[0] == 1 and hg_lb_param.shape[0] == 2, "single-layer block"
    i = 0
    w_rt = jnp.zeros((d, ROUTE_COLS), F32)
    w_rt = w_rt.at[:, 0:N_GROUPS].set(w_grp[i])
    w_rt = w_rt.at[:, EXP_ROW0:EXP_ROW0 + N_EXPERTS].set(w_exp[i])
    w_rt_hi = w_rt.astype(BF16)
    w_rt_lo = (w_rt - w_rt_hi.astype(F32)).astype(BF16)
    w_rt2 = jnp.concatenate([w_rt_hi, w_rt_lo], axis=1)
    bfull = jnp.repeat(gm_b_sp[i].T, GM_WIDTH // GM_GROUPS, axis=1)

    x1, h2, logits = _mixer(
        x, row(g_mix[i]), w_in[i].astype(BF16), row(gm_ln_g[i]), row(gm_ln_b[i]), gm_w_sp[i],
        bfull, w_up_a[i].astype(BF16), hg_lb_param, row(hg_norm_g[i]),
        w_up_b[i].astype(BF16), w_out[i].astype(BF16), row(g_ffn[i]), w_rt2, w_rt_hi)

    n_rows = 2 * t + N_EXPERTS * FFN_BLOCK
    nb = n_rows // FFN_BLOCK
    nbp = -(-nb // LANES) * LANES
    dest, gt, binfo = _route(logits.reshape(t, ROUTE_COLS), nbp)
    buf = _scatter(dest, h2, n_rows)
    ybuf = _ffn(binfo[0, :nb], buf, w1[i], w3[i], w2[i])
    out = _final(dest, x1.reshape(t, d), p[i].reshape(t, -1), gt, ybuf, row(g_ple[i]),
                 w_pg[i].astype(BF16), w_ple[i].astype(BF16), row(g_final))
    return out.reshape(bsz, seq, d)
```

```python
import functools

import jax
import jax.numpy as jnp
from jax import lax
from jax.experimental import pallas as pl
from jax.experimental.pallas import tpu as pltpu

F32 = jnp.float32
BF16 = jnp.bfloat16

EPS = 1e-6
GM_WIDTH = 512
GM_GROUPS = 8
GM_CHUNK = 128
HG_HEADS = 4
HG_KEY = 128
HG_CHUNK = 64
HG_KW = HG_HEADS * HG_KEY
N_GROUPS = 4
EXP_PER_GROUP = 8
N_EXPERTS = N_GROUPS * EXP_PER_GROUP
D_EXPERT = 512
LANES = 128
ROUTE_COLS = LANES
EXP_ROW0 = 8

MIX_TILE = 512
ROUTE_TILE = 512
FFN_BLOCK = 256
VMEM_LIMIT = 56 * 1024 * 1024


def _rms(x, g):
    return x * lax.rsqrt(jnp.mean(x * x, axis=-1, keepdims=True) + EPS) * g


def _dot(a, b):
    return jnp.dot(a, b, preferred_element_type=F32)


def _dot_nt(a, b):
    return lax.dot_general(a, b, (((1,), (1,)), ((), ())), preferred_element_type=F32)


def _dot_tn(a, b):
    return lax.dot_general(a, b, (((0,), (0,)), ((), ())), preferred_element_type=F32)


ROW_TILE = 8


def _store_token_rows(ref, val):
    n = val.shape[0]
    for c in range(ROW_TILE):
        ref[pl.ds(c, n, stride=ROW_TILE), :] = val[:, c * LANES:(c + 1) * LANES]


def _load_token_rows(ref, n, first=0):
    return jnp.concatenate(
        [ref[pl.ds(first * ROW_TILE + c, n, stride=ROW_TILE), :] for c in range(ROW_TILE)], axis=1)


def _mixer_kernel(x_ref, gmix_ref, win_ref, lng_ref, lnb_ref, wsp_ref, bfull_ref, wupa_ref,
                  lbp_ref, ng_ref, wupb_ref, wout_ref, gffn_ref, wrt2_ref, wrt1_ref,
                  x1_ref, h2_ref, lg_ref,
                  z_ref, h_ref, ya_ref, yb_ref, m_ref, st_ref):
    ts = x_ref.shape[1]
    in_cols = win_ref.shape[1]

    @pl.when(pl.program_id(1) == 0)
    def _():
        st_ref[...] = jnp.zeros_like(st_ref)

    h_ref[...] = _rms(x_ref[0], gmix_ref[...]).astype(BF16)
    for n in range(0, in_cols, 512):
        z_ref[:, n:n + 512] = _dot(h_ref[...], win_ref[:, n:n + 512])

    row = lax.broadcasted_iota(jnp.int32, (GM_CHUNK, GM_CHUNK), 0)
    col = lax.broadcasted_iota(jnp.int32, (GM_CHUNK, GM_CHUNK), 1)
    tril = row >= col
    low_half = col < (LANES // 2)

    def gm_body(c, carry):
        r0 = c * GM_CHUNK
        u = jax.nn.gelu(z_ref[pl.ds(r0, GM_CHUNK), 0:GM_WIDTH])
        v = jax.nn.gelu(z_ref[pl.ds(r0, GM_CHUNK), GM_WIDTH:2 * GM_WIDTH])
        d = v - jnp.mean(v, axis=-1, keepdims=True)
        var = jnp.mean(d * d, axis=-1, keepdims=True)
        vn = d * lax.rsqrt(var + EPS) * lng_ref[...] + lnb_ref[...]
        parts = []
        for p in range(GM_GROUPS // 2):
            vp = vn[:, p * LANES:(p + 1) * LANES]
            v_lo = jnp.where(low_half, vp, 0.0).astype(BF16)
            v_hi = jnp.where(low_half, 0.0, vp).astype(BF16)
            w_lo = jnp.where(tril, wsp_ref[2 * p], 0.0).astype(BF16)
            w_hi = jnp.where(tril, wsp_ref[2 * p + 1], 0.0).astype(BF16)
            parts.append(_dot(w_lo, v_lo) + _dot(w_hi, v_hi))
        s = jnp.concatenate(parts, axis=1) + bfull_ref[...]
        ya_ref[pl.ds(r0, GM_CHUNK), :] = (u * s).astype(BF16)
        return carry

    for c in range(ts // GM_CHUNK):
        gm_body(c, 0)

    lbp = lbp_ref[...]
    lmax = jnp.maximum(lbp[0:1], lbp[1:2])
    e0 = jnp.exp(lbp[0:1] - lmax)
    e1 = jnp.exp(lbp[1:2] - lmax)
    lb = e0 / (e0 + e1)

    crow = lax.broadcasted_iota(jnp.int32, (HG_CHUNK, HG_CHUNK), 0)
    ccol = lax.broadcasted_iota(jnp.int32, (HG_CHUNK, HG_CHUNK), 1)
    ctril = crow >= ccol
    tri = jnp.where(ctril, 1.0, 0.0).astype(BF16)
    q0 = 2 * GM_WIDTH

    def hg_body(c, carry):
        r0 = c * HG_CHUNK
        zq = z_ref[pl.ds(r0, HG_CHUNK), q0:q0 + HG_KW]
        zf = z_ref[pl.ds(r0, HG_CHUNK), q0 + HG_KW:q0 + 2 * HG_KW]
        vi = z_ref[pl.ds(r0, HG_CHUNK), q0 + 2 * HG_KW:q0 + 3 * HG_KW]
        zog = z_ref[pl.ds(r0, HG_CHUNK), q0 + 3 * HG_KW:q0 + 4 * HG_KW]
        qf = zq * jax.nn.sigmoid(zq)
        f = lb + (1.0 - lb) * jax.nn.sigmoid(zf)
        logf = jnp.log(f)
        k = 1.0 - f
        lhi = logf.astype(BF16)
        llo = (logf - lhi.astype(F32)).astype(BF16)
        b = _dot(tri, lhi) + _dot(tri, llo)
        b_last = b[HG_CHUNK - 1:HG_CHUNK, :]
        q_dec = (qf * jnp.exp(b)).astype(BF16)
        k_inv = (k * jnp.exp(-b)).astype(BF16)
        k_end = (k * jnp.exp(b_last - b)).astype(BF16)
        decay = jnp.exp(b_last)
        vb = vi.astype(BF16)
        og = jax.nn.sigmoid(zog)
        outs = []
        for hh in range(HG_HEADS):
            sl = slice(hh * HG_KEY, (hh + 1) * HG_KEY)
            att = jnp.where(ctril, _dot_nt(q_dec[:, sl], k_inv[:, sl]), 0.0)
            st = st_ref[hh]
            o = _dot(att.astype(BF16), vb[:, sl]) + _dot_nt(q_dec[:, sl], st.astype(BF16))
            st_ref[hh] = st * decay[:, sl] + _dot_tn(vb[:, sl], k_end[:, sl])
            o = o * lax.rsqrt(jnp.mean(o * o, axis=-1, keepdims=True) + EPS) * ng_ref[:, sl]
            outs.append(o * og[:, sl])
        yb_ref[pl.ds(r0, HG_CHUNK), :] = jnp.concatenate(outs, axis=1).astype(BF16)
        return carry

    for c in range(ts // HG_CHUNK):
        hg_body(c, 0)

    ga0 = q0 + 4 * HG_KW
    gb0 = ga0 + x_ref.shape[2]
    for n in range(0, x_ref.shape[2], 512):
        ua = _dot(ya_ref[...], wupa_ref[:, n:n + 512])
        ub = _dot(yb_ref[...], wupb_ref[:, n:n + 512])
        merged = (jax.nn.sigmoid(z_ref[:, ga0 + n:ga0 + n + 512]) * ua
                  + jax.nn.sigmoid(z_ref[:, gb0 + n:gb0 + n + 512]) * ub)
        m_ref[:, n:n + 512] = merged.astype(BF16)
    x1 = x_ref[0] + _dot(m_ref[...], wout_ref[...])
    x1_ref[0] = x1
    h2 = _rms(x1, gffn_ref[...])
    _store_token_rows(h2_ref, h2)
    hi = h2.astype(BF16)
    lo = (h2 - hi.astype(F32)).astype(BF16)
    l2 = _dot(hi, wrt2_ref[...])
    lg_ref[0] = l2[:, :ROUTE_COLS] + l2[:, ROUTE_COLS:] + _dot(lo, wrt1_ref[...])


def _mixer(x, g_mix, w_in, ln_g, ln_b, w_sp, bfull, w_up_a, lbp, norm_g, w_up_b, w_out, g_ffn,
           w_rt2, w_rt1):
    bsz, seq, d = x.shape
    ts = min(MIX_TILE, seq)
    in_cols = w_in.shape[1]

    def full(a):
        nd = a.ndim
        return pl.BlockSpec(a.shape, lambda b, j, _nd=nd: (0,) * _nd, pipeline_mode=pl.Buffered(1))

    consts = (g_mix, w_in, ln_g, ln_b, w_sp, bfull, w_up_a, lbp, norm_g, w_up_b, w_out, g_ffn,
              w_rt2, w_rt1)
    assert d == ROW_TILE * LANES
    nj = seq // ts
    tile = lambda w: pl.BlockSpec((1, ts, w), lambda b, j: (b, j, 0))
    return pl.pallas_call(
        _mixer_kernel,
        grid=(bsz, nj),
        in_specs=[tile(d)] + [full(a) for a in consts],
        out_specs=[tile(d),
                   pl.BlockSpec((ts * ROW_TILE, LANES), lambda b, j: (b * nj + j, 0)),
                   tile(ROUTE_COLS)],
        out_shape=[jax.ShapeDtypeStruct((bsz, seq, d), F32),
                   jax.ShapeDtypeStruct((bsz * seq * ROW_TILE, LANES), F32),
                   jax.ShapeDtypeStruct((bsz, seq, ROUTE_COLS), F32)],
        scratch_shapes=[pltpu.VMEM((ts, in_cols), F32),
                        pltpu.VMEM((ts, d), BF16),
                        pltpu.VMEM((ts, GM_WIDTH), BF16),
                        pltpu.VMEM((ts, HG_KW), BF16),
                        pltpu.VMEM((ts, d), BF16),
                        pltpu.VMEM((HG_HEADS, HG_KEY, HG_KEY), F32)],
        compiler_params=pltpu.CompilerParams(
            dimension_semantics=("arbitrary", "arbitrary"), vmem_limit_bytes=VMEM_LIMIT),
        name="mixer",
    )(x, *consts)


def _route_kernel(lg_ref, utri_ref, dest_ref, gt_ref, binfo_ref, rec_ref, cnt_ref):
    phase = pl.program_id(0)
    i = pl.program_id(1)
    r = lg_ref.shape[0]
    sub8 = lax.broadcasted_iota(jnp.int32, (8, r), 0)
    e32 = lax.broadcasted_iota(jnp.int32, (N_EXPERTS, r), 0).astype(F32)

    @pl.when(jnp.logical_and(phase == 0, i == 0))
    def _():
        cnt_ref[...] = jnp.zeros_like(cnt_ref)

    @pl.when(phase == 0)
    def _():
        lt = lg_ref[...].T
        l0, l1, l2, l3 = lt[0:1], lt[1:2], lt[2:3], lt[3:4]
        gmax = jnp.maximum(jnp.maximum(l0, l1), jnp.maximum(l2, l3))
        gsum = (jnp.exp(l0 - gmax) + jnp.exp(l1 - gmax)) + (jnp.exp(l2 - gmax) + jnp.exp(l3 - gmax))
        p_g = 1.0 / gsum
        gsel = jnp.where(l0 == gmax, 0, jnp.where(l1 == gmax, 1, jnp.where(l2 == gmax, 2, 3)))
        eg = [lt[EXP_ROW0 + EXP_PER_GROUP * g:EXP_ROW0 + EXP_PER_GROUP * (g + 1)]
              for g in range(N_GROUPS)]
        sel = jnp.where(gsel == 0, eg[0], jnp.where(gsel == 1, eg[1],
                                                     jnp.where(gsel == 2, eg[2], eg[3])))
        m1 = jnp.max(sel, axis=0, keepdims=True)
        i1 = jnp.min(jnp.where(sel == m1, sub8, EXP_PER_GROUP), axis=0, keepdims=True)
        sel2 = jnp.where(sub8 == i1, -jnp.inf, sel)
        m2 = jnp.max(sel2, axis=0, keepdims=True)
        i2 = jnp.min(jnp.where(sel2 == m2, sub8, EXP_PER_GROUP), axis=0, keepdims=True)
        ex = jnp.exp(m2 - m1)
        g1 = 1.0 / (1.0 + ex)
        g2 = ex * g1
        eid1 = (gsel * EXP_PER_GROUP + i1).astype(F32)
        eid2 = (gsel * EXP_PER_GROUP + i2).astype(F32)
        oh1 = jnp.where(e32 == eid1, 1.0, 0.0)
        oh2 = jnp.where(e32 == eid2, 1.0, 0.0)
        cum1 = _dot(oh1.astype(BF16), utri_ref[...])
        cum2 = _dot(oh2.astype(BF16), utri_ref[...])
        tot1 = jnp.sum(oh1, axis=1, keepdims=True)
        tot2 = jnp.sum(oh2, axis=1, keepdims=True)
        base = cnt_ref[:, 0:1]
        rank1 = jnp.sum(oh1 * (base + cum1), axis=0, keepdims=True)
        rank2 = jnp.sum(oh2 * (base + tot1 + cum2), axis=0, keepdims=True)
        cnt_ref[...] = cnt_ref[...] + (tot1 + tot2)
        rec = jnp.where(sub8 == 0, eid1, jnp.where(sub8 == 1, eid2, jnp.where(
            sub8 == 2, rank1, jnp.where(sub8 == 3, rank2, jnp.where(
                sub8 == 4, g1 * p_g, jnp.where(sub8 == 5, g2 * p_g, 0.0))))))
        rec_ref[i] = rec

    @pl.when(phase == 1)
    def _():
        cnt = cnt_ref[...]
        padded = jnp.floor((cnt + (FFN_BLOCK - 1)) * (1.0 / FFN_BLOCK)) * FFN_BLOCK
        rows = lax.broadcasted_iota(jnp.int32, cnt.shape, 0)
        pad_end = padded
        for s in (1, 2, 4, 8, 16):
            pad_end = pad_end + jnp.where(rows >= s, pltpu.roll(pad_end, s, axis=0), 0.0)
        pad_start = (pad_end - padded)[:, 0:1]
        rec = rec_ref[i]
        d1 = rec[2:3] + jnp.sum(jnp.where(e32 == rec[0:1], pad_start, 0.0), axis=0, keepdims=True)
        d2 = rec[3:4] + jnp.sum(jnp.where(e32 == rec[1:2], pad_start, 0.0), axis=0, keepdims=True)
        dest_ref[0] = jnp.where(sub8 == 0, d1, jnp.where(sub8 == 1, d2, 0.0)).astype(jnp.int32)
        subw = lax.broadcasted_iota(jnp.int32, (ROUTE_COLS, r), 0)
        gates = jnp.where(subw == 0, rec[4:5], jnp.where(subw == 1, rec[5:6], 0.0))
        gt_ref[...] = gates.T
        nbp = binfo_ref.shape[1]
        blk_start = (lax.broadcasted_iota(jnp.int32, (N_EXPERTS, nbp), 1) * FFN_BLOCK).astype(F32)
        n_le = jnp.sum(jnp.where(pad_end[:, 0:1] <= blk_start, 1.0, 0.0), axis=0, keepdims=True)
        binfo_ref[...] = jnp.broadcast_to(n_le, binfo_ref.shape).astype(jnp.int32)


def _route(logits, nbp):
    t = logits.shape[0]
    r = min(ROUTE_TILE, t)
    nt = t // r
    utri = jnp.triu(jnp.ones((r, r), F32), k=1).astype(BF16)
    return pl.pallas_call(
        _route_kernel,
        grid=(2, nt),
        in_specs=[pl.BlockSpec((r, ROUTE_COLS), lambda p, i: (i * (1 - p) + (nt - 1) * p, 0)),
                  pl.BlockSpec((r, r), lambda p, i: (0, 0))],
        out_specs=[pl.BlockSpec((1, 8, r), lambda p, i: (i * p, 0, 0)),
                   pl.BlockSpec((r, ROUTE_COLS), lambda p, i: (i * p, 0)),
                   pl.BlockSpec((8, nbp), lambda p, i: (0, 0))],
        out_shape=[jax.ShapeDtypeStruct((nt, 8, r), jnp.int32),
                   jax.ShapeDtypeStruct((t, ROUTE_COLS), F32),
                   jax.ShapeDtypeStruct((8, nbp), jnp.int32)],
        scratch_shapes=[pltpu.VMEM((nt, 8, r), F32),
                        pltpu.VMEM((N_EXPERTS, LANES), F32)],
        compiler_params=pltpu.CompilerParams(
            dimension_semantics=("arbitrary", "arbitrary"), vmem_limit_bytes=VMEM_LIMIT),
        name="route",
    )(logits, utri)


DMA_UNROLL = 8


def _token_copy(src, src_tok, dst, dst_tok, sem):
    return pltpu.make_async_copy(
        src.at[pl.ds(pl.multiple_of(src_tok * ROW_TILE, ROW_TILE), ROW_TILE)],
        dst.at[pl.ds(pl.multiple_of(dst_tok * ROW_TILE, ROW_TILE), ROW_TILE)], sem)


def _scatter_kernel(dest_ref, h2_ref, buf_in_hbm, buf_hbm, sem):
    del buf_in_hbm
    r = dest_ref.shape[0] // 2

    def issue(g, carry):
        t0 = g * DMA_UNROLL
        idx = [[dest_ref[k * r + t0 + u] for k in range(2)] for u in range(DMA_UNROLL)]
        for u in range(DMA_UNROLL):
            for k in range(2):
                _token_copy(h2_ref, t0 + u, buf_hbm, idx[u][k], sem).start()
        return carry

    lax.fori_loop(0, r // DMA_UNROLL, issue, 0)

    def drain(t, carry):
        _token_copy(h2_ref, 0, buf_hbm, 0, sem).wait()
        return carry

    lax.fori_loop(0, 2 * r, drain, 0, unroll=DMA_UNROLL)


def _scatter(dest, r, h2, n_rows):
    nt = dest.shape[0] // (2 * r)
    buf0 = jnp.zeros((n_rows * ROW_TILE, LANES), F32)
    return pl.pallas_call(
        _scatter_kernel,
        grid=(nt,),
        in_specs=[pl.BlockSpec((2 * r,), lambda i: (i,), memory_space=pltpu.SMEM),
                  pl.BlockSpec((r * ROW_TILE, LANES), lambda i: (i, 0)),
                  pl.BlockSpec(memory_space=pl.ANY)],
        out_specs=pl.BlockSpec(memory_space=pl.ANY),
        out_shape=jax.ShapeDtypeStruct(buf0.shape, F32),
        scratch_shapes=[pltpu.SemaphoreType.DMA(())],
        input_output_aliases={2: 0},
        compiler_params=pltpu.CompilerParams(
            dimension_semantics=("arbitrary",), has_side_effects=True),
        name="scatter",
    )(dest, h2, buf0)


def _ffn_kernel(be_ref, x_ref, w1_ref, w3_ref, w2_ref, y_ref, w13_s, w2_s):
    i = pl.program_id(0)
    e = be_ref[i]
    used = e < N_EXPERTS
    fresh = jnp.logical_or(i == 0, e != be_ref[jnp.maximum(i - 1, 0)])

    @pl.when(jnp.logical_and(used, fresh))
    def _():
        w13_s[:, :D_EXPERT] = w1_ref[0].astype(BF16)
        w13_s[:, D_EXPERT:] = w3_ref[0].astype(BF16)
        w2_s[...] = w2_ref[0].astype(BF16)

    @pl.when(used)
    def _():
        h13 = _dot(_load_token_rows(x_ref, FFN_BLOCK).astype(BF16), w13_s[...])
        a = h13[:, :D_EXPERT]
        act = (a * jax.nn.sigmoid(a) * h13[:, D_EXPERT:]).astype(BF16)
        _store_token_rows(y_ref, _dot(act, w2_s[...]))

    @pl.when(jnp.logical_not(used))
    def _():
        y_ref[...] = jnp.zeros_like(y_ref)


def _ffn(blk_e, buf, w1, w3, w2):
    nb = buf.shape[0] // (FFN_BLOCK * ROW_TILE)
    d = w1.shape[1]
    emap = lambda i, be: (jnp.minimum(be[i], N_EXPERTS - 1), 0, 0)
    rows = pl.BlockSpec((FFN_BLOCK * ROW_TILE, LANES), lambda i, be: (i, 0))
    return pl.pallas_call(
        _ffn_kernel,
        grid_spec=pltpu.PrefetchScalarGridSpec(
            num_scalar_prefetch=1,
            grid=(nb,),
            in_specs=[rows,
                      pl.BlockSpec((1, d, D_EXPERT), emap),
                      pl.BlockSpec((1, d, D_EXPERT), emap),
                      pl.BlockSpec((1, D_EXPERT, d), emap)],
            out_specs=rows,
            scratch_shapes=[pltpu.VMEM((d, 2 * D_EXPERT), BF16),
                            pltpu.VMEM((D_EXPERT, d), BF16)]),
        out_shape=jax.ShapeDtypeStruct(buf.shape, F32),
        compiler_params=pltpu.CompilerParams(
            dimension_semantics=("arbitrary",), vmem_limit_bytes=VMEM_LIMIT),
        name="ffn",
    )(blk_e, buf, w1, w3, w2)


def _final_kernel(dest_ref, dnext_ref, x1_ref, p_ref, gt_ref, ybuf_hbm, gple_ref, wpg_ref,
                  wple_ref, gfin_ref, out_ref, y_ref, sems):
    i = pl.program_id(0)
    r = x1_ref.shape[0]
    slot = i % 2

    def issue_group(d_ref, s, t0):
        idx = [[d_ref[k * r + t0 + u] for k in range(2)] for u in range(DMA_UNROLL)]
        for u in range(DMA_UNROLL):
            for k in range(2):
                _token_copy(ybuf_hbm, idx[u][k], y_ref.at[s, k], t0 + u, sems.at[s]).start()

    def drain(s):
        def body(t, carry):
            _token_copy(ybuf_hbm, 0, y_ref.at[s, 0], 0, sems.at[s]).wait()
            return carry

        lax.fori_loop(0, 2 * r, body, 0, unroll=DMA_UNROLL)

    @pl.when(i == 0)
    def _():
        def body(g, carry):
            issue_group(dest_ref, 0, g * DMA_UNROLL)
            return carry

        lax.fori_loop(0, r // DMA_UNROLL, body, 0)

    drain(slot)

    n_batch = 4
    per = r // n_batch

    def issue_batch(b):
        for t0 in range(b * per, (b + 1) * per, DMA_UNROLL):
            issue_group(dnext_ref, 1 - slot, t0)

    issue_batch(0)
    x2 = (x1_ref[...] + gt_ref[:, 0:1] * _load_token_rows(y_ref.at[slot, 0], r)
          + gt_ref[:, 1:2] * _load_token_rows(y_ref.at[slot, 1], r))
    h3 = _rms(x2, gple_ref[...]).astype(BF16)
    issue_batch(1)
    gate = jax.nn.sigmoid(_dot(h3, wpg_ref[...]))
    issue_batch(2)
    pe = _dot(p_ref[...].astype(BF16), wple_ref[...])
    issue_batch(3)
    x3 = x2 + gate * pe
    out_ref[...] = _rms(x3, gfin_ref[...])

    @pl.when(i + 1 == pl.num_programs(0))
    def _():
        drain(1 - slot)


def _final(dest, r, x1, p, gt, ybuf, g_ple, w_pg, w_ple, g_final):
    t, d = x1.shape
    nt = t // r

    def full(a):
        nd = a.ndim
        return pl.BlockSpec(a.shape, lambda i, _nd=nd: (0,) * _nd, pipeline_mode=pl.Buffered(1))

    return pl.pallas_call(
        _final_kernel,
        grid=(nt,),
        in_specs=[pl.BlockSpec((2 * r,), lambda i: (i,), memory_space=pltpu.SMEM),
                  pl.BlockSpec((2 * r,), lambda i: (jnp.minimum(i + 1, nt - 1),),
                               memory_space=pltpu.SMEM),
                  pl.BlockSpec((r, d), lambda i: (i, 0)),
                  pl.BlockSpec((r, p.shape[1]), lambda i: (i, 0)),
                  pl.BlockSpec((r, ROUTE_COLS), lambda i: (i, 0)),
                  pl.BlockSpec(memory_space=pl.ANY),
                  full(g_ple), full(w_pg), full(w_ple), full(g_final)],
        out_specs=pl.BlockSpec((r, d), lambda i: (i, 0)),
        out_shape=jax.ShapeDtypeStruct((t, d), F32),
        scratch_shapes=[pltpu.VMEM((2, 2, r * ROW_TILE, LANES), F32),
                        pltpu.SemaphoreType.DMA((2,))],
        compiler_params=pltpu.CompilerParams(
            dimension_semantics=("arbitrary",), vmem_limit_bytes=VMEM_LIMIT),
        name="final",
    )(dest, dest, x1, p, gt, ybuf, g_ple, w_pg, w_ple, g_final)


def kernel(x, p, g_mix, w_in, gm_ln_g, gm_ln_b, gm_w_sp, gm_b_sp, w_up_a, hg_lb_param, hg_norm_g,
           w_up_b, w_out, g_ffn, w_grp, w_exp, w1, w3, w2, g_ple, w_pg, w_ple, g_final):
    bsz, seq, d = x.shape
    t = bsz * seq
    row = lambda a: a.reshape(1, -1)

    assert w_in.shape[0] == 1 and hg_lb_param.shape[0] == 2, "single-layer block"
    i = 0
    w_rt = jnp.zeros((d, ROUTE_COLS), F32)
    w_rt = w_rt.at[:, 0:N_GROUPS].set(w_grp[i])
    w_rt = w_rt.at[:, EXP_ROW0:EXP_ROW0 + N_EXPERTS].set(w_exp[i])
    w_rt_hi = w_rt.astype(BF16)
    w_rt_lo = (w_rt - w_rt_hi.astype(F32)).astype(BF16)
    w_rt2 = jnp.concatenate([w_rt_hi, w_rt_lo], axis=1)
    bfull = jnp.repeat(gm_b_sp[i].T, GM_WIDTH // GM_GROUPS, axis=1)

    x1, h2, logits = _mixer(
        x, row(g_mix[i]), w_in[i].astype(BF16), row(gm_ln_g[i]), row(gm_ln_b[i]), gm_w_sp[i],
        bfull, w_up_a[i].astype(BF16), hg_lb_param, row(hg_norm_g[i]),
        w_up_b[i].astype(BF16), w_out[i].astype(BF16), row(g_ffn[i]), w_rt2, w_rt_hi)

    n_rows = 2 * t + N_EXPERTS * FFN_BLOCK
    nb = n_rows // FFN_BLOCK
    nbp = -(-nb // LANES) * LANES
    dest, gt, binfo = _route(logits.reshape(t, ROUTE_COLS), nbp)
    r = dest.shape[2]
    dest = dest[:, 0:2, :].reshape(-1)
    buf = _scatter(dest, r, h2, n_rows)
    ybuf = _ffn(binfo[0, :nb], buf, w1[i], w3[i], w2[i])
    out = _final(dest, r, x1.reshape(t, d), p[i].reshape(t, -1), gt, ybuf, row(g_ple[i]),
                 w_pg[i].astype(BF16), w_ple[i].astype(BF16), row(g_final))
    return out.reshape(bsz, seq, d)
```

```python
import functools

import jax
import jax.numpy as jnp
from jax import lax
from jax.experimental import pallas as pl
from jax.experimental.pallas import tpu as pltpu

F32 = jnp.float32
BF16 = jnp.bfloat16

EPS = 1e-6
GM_WIDTH = 512
GM_GROUPS = 8
GM_CHUNK = 128
HG_HEADS = 4
HG_KEY = 128
HG_CHUNK = 64
HG_KW = HG_HEADS * HG_KEY
N_GROUPS = 4
EXP_PER_GROUP = 8
N_EXPERTS = N_GROUPS * EXP_PER_GROUP
D_EXPERT = 512
LANES = 128
ROUTE_COLS = LANES
EXP_ROW0 = 8

MIX_TILE = 512
ROUTE_TILE = 512
FFN_BLOCK = 512
VMEM_LIMIT = 56 * 1024 * 1024


def _rms(x, g):
    return x * lax.rsqrt(jnp.mean(x * x, axis=-1, keepdims=True) + EPS) * g


def _dot(a, b):
    return jnp.dot(a, b, preferred_element_type=F32)


def _dot_nt(a, b):
    return lax.dot_general(a, b, (((1,), (1,)), ((), ())), preferred_element_type=F32)


def _dot_tn(a, b):
    return lax.dot_general(a, b, (((0,), (0,)), ((), ())), preferred_element_type=F32)


ROW_TILE = 8


def _store_token_rows(ref, val):
    n = val.shape[0]
    for c in range(ROW_TILE):
        ref[pl.ds(c, n, stride=ROW_TILE), :] = val[:, c * LANES:(c + 1) * LANES]


def _load_token_rows(ref, n, first=0):
    return jnp.concatenate(
        [ref[pl.ds(first * ROW_TILE + c, n, stride=ROW_TILE), :] for c in range(ROW_TILE)], axis=1)


def _mixer_kernel(x_ref, gmix_ref, win_ref, lng_ref, lnb_ref, wsp_ref, bfull_ref, wupa_ref,
                  lbp_ref, ng_ref, wupb_ref, wout_ref, gffn_ref, wrt2_ref, wrt1_ref,
                  x1_ref, h2_ref, lg_ref,
                  z_ref, h_ref, ya_ref, yb_ref, m_ref, st_ref, qd_ref, oi_ref, ds_ref, sc_ref,
                  dec_ref, b_ref, ki_ref, ke_ref, att_ref):
    ts = x_ref.shape[1]
    in_cols = win_ref.shape[1]

    @pl.when(pl.program_id(1) == 0)
    def _():
        st_ref[...] = jnp.zeros_like(st_ref)

    h_ref[...] = _rms(x_ref[0], gmix_ref[...]).astype(BF16)
    def in_proj(col0):
        for n in range(col0, col0 + 1024, 512):
            z_ref[:, n:n + 512] = _dot(h_ref[...], win_ref[:, n:n + 512])

    row = lax.broadcasted_iota(jnp.int32, (GM_CHUNK, GM_CHUNK), 0)
    col = lax.broadcasted_iota(jnp.int32, (GM_CHUNK, GM_CHUNK), 1)
    tril = row >= col
    low_half = col < (LANES // 2)
    ucols = slice(0, GM_WIDTH)
    ncols = slice(GM_WIDTH, 2 * GM_WIDTH)

    def gm_norm(c):
        rows = pl.ds(c * GM_CHUNK, GM_CHUNK)
        z_ref[rows, ucols] = jax.nn.gelu(z_ref[rows, ucols])
        v = jax.nn.gelu(z_ref[rows, ncols])
        d = v - jnp.mean(v, axis=-1, keepdims=True)
        var = jnp.mean(d * d, axis=-1, keepdims=True)
        z_ref[rows, ncols] = d * lax.rsqrt(var + EPS) * lng_ref[...] + lnb_ref[...]

    def gm_gate(c):
        rows = pl.ds(c * GM_CHUNK, GM_CHUNK)
        parts = []
        for p in range(GM_GROUPS // 2):
            vp = z_ref[rows, GM_WIDTH + p * LANES:GM_WIDTH + (p + 1) * LANES]
            v_lo = jnp.where(low_half, vp, 0.0).astype(BF16)
            v_hi = jnp.where(low_half, 0.0, vp).astype(BF16)
            w_lo = jnp.where(tril, wsp_ref[2 * p], 0.0).astype(BF16)
            w_hi = jnp.where(tril, wsp_ref[2 * p + 1], 0.0).astype(BF16)
            parts.append(_dot(w_lo, v_lo) + _dot(w_hi, v_hi))
        s = jnp.concatenate(parts, axis=1) + bfull_ref[...]
        ya_ref[rows, :] = (z_ref[rows, ucols] * s).astype(BF16)

    lbp = lbp_ref[...]
    lmax = jnp.maximum(lbp[0:1], lbp[1:2])
    e0 = jnp.exp(lbp[0:1] - lmax)
    e1 = jnp.exp(lbp[1:2] - lmax)
    lb = e0 / (e0 + e1)

    blk = 2 * HG_CHUNK
    brow = lax.broadcasted_iota(jnp.int32, (blk, blk), 0)
    bcol = lax.broadcasted_iota(jnp.int32, (blk, blk), 1)
    causal = jnp.logical_and(brow >= bcol, (brow < HG_CHUNK) == (bcol < HG_CHUNK))
    tri = jnp.where(causal, 1.0, 0.0).astype(BF16)
    first_chunk = lax.broadcasted_iota(jnp.int32, (blk, HG_KW), 0) < HG_CHUNK
    q0 = 2 * GM_WIDTH
    n_chunks = ts // HG_CHUNK

    n_rb = ts // blk
    qcols = slice(q0, q0 + HG_KW)
    fcols = slice(q0 + HG_KW, q0 + 2 * HG_KW)
    vcols = slice(q0 + 2 * HG_KW, q0 + 3 * HG_KW)
    heads = [slice(hh * HG_KEY, (hh + 1) * HG_KEY) for hh in range(HG_HEADS)]

    def hg_cumsum(rb):
        rows = pl.ds(rb * blk, blk)
        zq = z_ref[rows, qcols]
        f = lb + (1.0 - lb) * jax.nn.sigmoid(z_ref[rows, fcols])
        logf = jnp.log(f)
        lhi = logf.astype(BF16)
        llo = (logf - lhi.astype(F32)).astype(BF16)
        b_ref[rows, :] = _dot(tri, lhi) + _dot(tri, llo)
        z_ref[rows, qcols] = zq * jax.nn.sigmoid(zq)
        z_ref[rows, fcols] = 1.0 - f

    def hg_decay(rb):
        rows = pl.ds(rb * blk, blk)
        b = b_ref[rows, :]
        qf = z_ref[rows, qcols]
        k = z_ref[rows, fcols]
        b_mid = b[HG_CHUNK - 1:HG_CHUNK, :]
        b_end = b[blk - 1:blk, :]
        b_last = jnp.where(first_chunk, b_mid, b_end)
        qd_ref[rows, :] = (qf * jnp.exp(b)).astype(BF16)
        ki_ref[rows, :] = (k * jnp.exp(-b)).astype(BF16)
        ke_ref[rows, :] = (k * jnp.exp(b_last - b)).astype(BF16)
        dec_ref[2 * rb:2 * rb + 1, :] = jnp.exp(b_mid)
        dec_ref[2 * rb + 1:2 * rb + 2, :] = jnp.exp(b_end)

    def hg_scores(rb):
        rows = pl.ds(rb * blk, blk)
        for hh, sl in enumerate(heads):
            att = _dot_nt(qd_ref[rows, sl], ki_ref[rows, sl])
            att_ref[rb, hh] = jnp.where(causal, att, 0.0).astype(BF16)

    def hg_values(rb):
        rows = pl.ds(rb * blk, blk)
        vb = z_ref[rows, vcols].astype(BF16)
        for hh, sl in enumerate(heads):
            oi_ref[rows, sl] = _dot(att_ref[rb, hh], vb[:, sl])
        for cc in range(2):
            crows = pl.ds(rb * blk + cc * HG_CHUNK, HG_CHUNK)
            for hh, sl in enumerate(heads):
                ds_ref[2 * rb + cc, hh] = _dot_tn(
                    vb[cc * HG_CHUNK:(cc + 1) * HG_CHUNK, sl], ke_ref[crows, sl])

    def hg_scan(hh):
        sl = slice(hh * HG_KEY, (hh + 1) * HG_KEY)
        st = st_ref[hh]
        for c in range(n_chunks):
            sc_ref[c, hh] = st.astype(BF16)
            st = st * dec_ref[c:c + 1, sl] + ds_ref[c, hh]
        st_ref[hh] = st

    def hg_out(c):
        rows = pl.ds(c * HG_CHUNK, HG_CHUNK)
        og = jax.nn.sigmoid(z_ref[rows, q0 + 3 * HG_KW:q0 + 4 * HG_KW])
        outs = []
        for hh in range(HG_HEADS):
            sl = slice(hh * HG_KEY, (hh + 1) * HG_KEY)
            o = oi_ref[rows, sl] + _dot_nt(qd_ref[rows, sl], sc_ref[c, hh])
            o = o * lax.rsqrt(jnp.mean(o * o, axis=-1, keepdims=True) + EPS) * ng_ref[:, sl]
            outs.append(o * og[:, sl])
        yb_ref[rows, :] = jnp.concatenate(outs, axis=1).astype(BF16)

    n_gm = ts // GM_CHUNK
    in_proj(q0)
    in_proj(q0 + 2 * HG_KW)
    for rb in range(n_rb):
        hg_cumsum(rb)
    in_proj(0)
    for rb in range(n_rb):
        hg_decay(rb)
    for rb in range(n_rb):
        hg_scores(rb)
    in_proj(q0 + 4 * HG_KW)
    for c in range(n_gm):
        gm_norm(c)
    for rb in range(n_rb):
        hg_values(rb)
    in_proj(q0 + 4 * HG_KW + 1024)
    for c in range(n_gm):
        gm_gate(c)
    for hh in range(HG_HEADS):
        hg_scan(hh)
    for c in range(n_chunks):
        hg_out(c)

    ga0 = q0 + 4 * HG_KW
    gb0 = ga0 + x_ref.shape[2]
    for n in range(0, x_ref.shape[2], 512):
        ua = _dot(ya_ref[...], wupa_ref[:, n:n + 512])
        ub = _dot(yb_ref[...], wupb_ref[:, n:n + 512])
        merged = (jax.nn.sigmoid(z_ref[:, ga0 + n:ga0 + n + 512]) * ua
                  + jax.nn.sigmoid(z_ref[:, gb0 + n:gb0 + n + 512]) * ub)
        m_ref[:, n:n + 512] = merged.astype(BF16)
    x1 = x_ref[0] + _dot(m_ref[...], wout_ref[...])
    x1_ref[0] = x1
    h2 = _rms(x1, gffn_ref[...])
    _store_token_rows(h2_ref, h2)
    hi = h2.astype(BF16)
    lo = (h2 - hi.astype(F32)).astype(BF16)
    l2 = _dot(hi, wrt2_ref[...])
    lg_ref[0] = l2[:, :ROUTE_COLS] + l2[:, ROUTE_COLS:] + _dot(lo, wrt1_ref[...])


def _mixer(x, g_mix, w_in, ln_g, ln_b, w_sp, bfull, w_up_a, lbp, norm_g, w_up_b, w_out, g_ffn,
           w_rt2, w_rt1):
    bsz, seq, d = x.shape
    ts = min(MIX_TILE, seq)
    in_cols = w_in.shape[1]

    def full(a):
        nd = a.ndim
        return pl.BlockSpec(a.shape, lambda b, j, _nd=nd: (0,) * _nd, pipeline_mode=pl.Buffered(1))

    consts = (g_mix, w_in, ln_g, ln_b, w_sp, bfull, w_up_a, lbp, norm_g, w_up_b, w_out, g_ffn,
              w_rt2, w_rt1)
    assert d == ROW_TILE * LANES
    nj = seq // ts
    tile = lambda w: pl.BlockSpec((1, ts, w), lambda b, j: (b, j, 0))
    return pl.pallas_call(
        _mixer_kernel,
        grid=(bsz, nj),
        in_specs=[tile(d)] + [full(a) for a in consts],
        out_specs=[tile(d),
                   pl.BlockSpec((ts * ROW_TILE, LANES), lambda b, j: (b * nj + j, 0)),
                   tile(ROUTE_COLS)],
        out_shape=[jax.ShapeDtypeStruct((bsz, seq, d), F32),
                   jax.ShapeDtypeStruct((bsz * seq * ROW_TILE, LANES), F32),
                   jax.ShapeDtypeStruct((bsz, seq, ROUTE_COLS), F32)],
        scratch_shapes=[pltpu.VMEM((ts, in_cols), F32),
                        pltpu.VMEM((ts, d), BF16),
                        pltpu.VMEM((ts, GM_WIDTH), BF16),
                        pltpu.VMEM((ts, HG_KW), BF16),
                        pltpu.VMEM((ts, d), BF16),
                        pltpu.VMEM((HG_HEADS, HG_KEY, HG_KEY), F32),
                        pltpu.VMEM((ts, HG_KW), BF16),
                        pltpu.VMEM((ts, HG_KW), F32),
                        pltpu.VMEM((ts // HG_CHUNK, HG_HEADS, HG_KEY, HG_KEY), F32),
                        pltpu.VMEM((ts // HG_CHUNK, HG_HEADS, HG_KEY, HG_KEY), BF16),
                        pltpu.VMEM((ts // HG_CHUNK, HG_KW), F32),
                        pltpu.VMEM((ts, HG_KW), F32),
                        pltpu.VMEM((ts, HG_KW), BF16),
                        pltpu.VMEM((ts, HG_KW), BF16),
                        pltpu.VMEM((ts // (2 * HG_CHUNK), HG_HEADS, 2 * HG_CHUNK, 2 * HG_CHUNK),
                                   BF16)],
        compiler_params=pltpu.CompilerParams(
            dimension_semantics=("arbitrary", "arbitrary"), vmem_limit_bytes=VMEM_LIMIT),
        name="mixer",
    )(x, *consts)


def _route_kernel(lg_ref, utri_ref, dest_ref, gt_ref, binfo_ref, rec_ref, cnt_ref):
    phase = pl.program_id(0)
    i = pl.program_id(1)
    r = lg_ref.shape[0]
    sub8 = lax.broadcasted_iota(jnp.int32, (8, r), 0)
    e32 = lax.broadcasted_iota(jnp.int32, (N_EXPERTS, r), 0).astype(F32)

    @pl.when(jnp.logical_and(phase == 0, i == 0))
    def _():
        cnt_ref[...] = jnp.zeros_like(cnt_ref)

    @pl.when(phase == 0)
    def _():
        lt = lg_ref[...].T
        l0, l1, l2, l3 = lt[0:1], lt[1:2], lt[2:3], lt[3:4]
        gmax = jnp.maximum(jnp.maximum(l0, l1), jnp.maximum(l2, l3))
        gsum = (jnp.exp(l0 - gmax) + jnp.exp(l1 - gmax)) + (jnp.exp(l2 - gmax) + jnp.exp(l3 - gmax))
        p_g = 1.0 / gsum
        gsel = jnp.where(l0 == gmax, 0, jnp.where(l1 == gmax, 1, jnp.where(l2 == gmax, 2, 3)))
        eg = [lt[EXP_ROW0 + EXP_PER_GROUP * g:EXP_ROW0 + EXP_PER_GROUP * (g + 1)]
              for g in range(N_GROUPS)]
        sel = jnp.where(gsel == 0, eg[0], jnp.where(gsel == 1, eg[1],
                                                     jnp.where(gsel == 2, eg[2], eg[3])))
        m1 = jnp.max(sel, axis=0, keepdims=True)
        i1 = jnp.min(jnp.where(sel == m1, sub8, EXP_PER_GROUP), axis=0, keepdims=True)
        sel2 = jnp.where(sub8 == i1, -jnp.inf, sel)
        m2 = jnp.max(sel2, axis=0, keepdims=True)
        i2 = jnp.min(jnp.where(sel2 == m2, sub8, EXP_PER_GROUP), axis=0, keepdims=True)
        ex = jnp.exp(m2 - m1)
        g1 = 1.0 / (1.0 + ex)
        g2 = ex * g1
        eid1 = (gsel * EXP_PER_GROUP + i1).astype(F32)
        eid2 = (gsel * EXP_PER_GROUP + i2).astype(F32)
        oh1 = jnp.where(e32 == eid1, 1.0, 0.0)
        oh2 = jnp.where(e32 == eid2, 1.0, 0.0)
        cum1 = _dot(oh1.astype(BF16), utri_ref[...])
        cum2 = _dot(oh2.astype(BF16), utri_ref[...])
        tot1 = jnp.sum(oh1, axis=1, keepdims=True)
        tot2 = jnp.sum(oh2, axis=1, keepdims=True)
        base = cnt_ref[:, 0:1]
        rank1 = jnp.sum(oh1 * (base + cum1), axis=0, keepdims=True)
        rank2 = jnp.sum(oh2 * (base + tot1 + cum2), axis=0, keepdims=True)
        cnt_ref[...] = cnt_ref[...] + (tot1 + tot2)
        rec = jnp.where(sub8 == 0, eid1, jnp.where(sub8 == 1, eid2, jnp.where(
            sub8 == 2, rank1, jnp.where(sub8 == 3, rank2, jnp.where(
                sub8 == 4, g1 * p_g, jnp.where(sub8 == 5, g2 * p_g, 0.0))))))
        rec_ref[i] = rec

    @pl.when(phase == 1)
    def _():
        cnt = cnt_ref[...]
        padded = jnp.floor((cnt + (FFN_BLOCK - 1)) * (1.0 / FFN_BLOCK)) * FFN_BLOCK
        rows = lax.broadcasted_iota(jnp.int32, cnt.shape, 0)
        pad_end = padded
        for s in (1, 2, 4, 8, 16):
            pad_end = pad_end + jnp.where(rows >= s, pltpu.roll(pad_end, s, axis=0), 0.0)
        pad_start = (pad_end - padded)[:, 0:1]
        rec = rec_ref[i]
        d1 = rec[2:3] + jnp.sum(jnp.where(e32 == rec[0:1], pad_start, 0.0), axis=0, keepdims=True)
        d2 = rec[3:4] + jnp.sum(jnp.where(e32 == rec[1:2], pad_start, 0.0), axis=0, keepdims=True)
        dest_ref[0] = jnp.where(sub8 == 0, d1, jnp.where(sub8 == 1, d2, 0.0)).astype(jnp.int32)
        subw = lax.broadcasted_iota(jnp.int32, (ROUTE_COLS, r), 0)
        gates = jnp.where(subw == 0, rec[4:5], jnp.where(subw == 1, rec[5:6], 0.0))
        gt_ref[...] = gates.T
        nbp = binfo_ref.shape[1]
        blk_start = (lax.broadcasted_iota(jnp.int32, (N_EXPERTS, nbp), 1) * FFN_BLOCK).astype(F32)
        n_le = jnp.sum(jnp.where(pad_end[:, 0:1] <= blk_start, 1.0, 0.0), axis=0, keepdims=True)
        binfo_ref[...] = jnp.broadcast_to(n_le, binfo_ref.shape).astype(jnp.int32)


def _route(logits, nbp):
    t = logits.shape[0]
    r = min(ROUTE_TILE, t)
    nt = t // r
    utri = jnp.triu(jnp.ones((r, r), F32), k=1).astype(BF16)
    return pl.pallas_call(
        _route_kernel,
        grid=(2, nt),
        in_specs=[pl.BlockSpec((r, ROUTE_COLS), lambda p, i: (i * (1 - p) + (nt - 1) * p, 0)),
                  pl.BlockSpec((r, r), lambda p, i: (0, 0))],
        out_specs=[pl.BlockSpec((1, 8, r), lambda p, i: (i * p, 0, 0)),
                   pl.BlockSpec((r, ROUTE_COLS), lambda p, i: (i * p, 0)),
                   pl.BlockSpec((8, nbp), lambda p, i: (0, 0))],
        out_shape=[jax.ShapeDtypeStruct((nt, 8, r), jnp.int32),
                   jax.ShapeDtypeStruct((t, ROUTE_COLS), F32),
                   jax.ShapeDtypeStruct((8, nbp), jnp.int32)],
        scratch_shapes=[pltpu.VMEM((nt, 8, r), F32),
                        pltpu.VMEM((N_EXPERTS, LANES), F32)],
        compiler_params=pltpu.CompilerParams(
            dimension_semantics=("arbitrary", "arbitrary"), vmem_limit_bytes=VMEM_LIMIT),
        name="route",
    )(logits, utri)


DMA_UNROLL = 8


def _token_copy(src, src_tok, dst, dst_tok, sem):
    return pltpu.make_async_copy(
        src.at[pl.ds(pl.multiple_of(src_tok * ROW_TILE, ROW_TILE), ROW_TILE)],
        dst.at[pl.ds(pl.multiple_of(dst_tok * ROW_TILE, ROW_TILE), ROW_TILE)], sem)


def _scatter_kernel(dest_ref, h2_ref, buf_in_hbm, buf_hbm, sem):
    del buf_in_hbm
    r = dest_ref.shape[0] // 2

    def issue(g, carry):
        t0 = g * DMA_UNROLL
        idx = [[dest_ref[k * r + t0 + u] for k in range(2)] for u in range(DMA_UNROLL)]
        for u in range(DMA_UNROLL):
            for k in range(2):
                _token_copy(h2_ref, t0 + u, buf_hbm, idx[u][k], sem).start(priority=k)
        return carry

    lax.fori_loop(0, r // DMA_UNROLL, issue, 0)

    def drain(t, carry):
        _token_copy(h2_ref, 0, buf_hbm, 0, sem).wait()
        return carry

    lax.fori_loop(0, 2 * r, drain, 0, unroll=DMA_UNROLL)


def _scatter(dest, r, h2, n_rows):
    nt = dest.shape[0] // (2 * r)
    buf0 = jnp.zeros((n_rows * ROW_TILE, LANES), F32)
    return pl.pallas_call(
        _scatter_kernel,
        grid=(nt,),
        in_specs=[pl.BlockSpec((2 * r,), lambda i: (i,), memory_space=pltpu.SMEM),
                  pl.BlockSpec((r * ROW_TILE, LANES), lambda i: (i, 0)),
                  pl.BlockSpec(memory_space=pl.ANY)],
        out_specs=pl.BlockSpec(memory_space=pl.ANY),
        out_shape=jax.ShapeDtypeStruct(buf0.shape, F32),
        scratch_shapes=[pltpu.SemaphoreType.DMA(())],
        input_output_aliases={2: 0},
        compiler_params=pltpu.CompilerParams(
            dimension_semantics=("arbitrary",), has_side_effects=True),
        name="scatter",
    )(dest, h2, buf0)


def _ffn_kernel(be_ref, x_ref, w1_ref, w3_ref, w2_ref, y_ref, w13_s, w2_s):
    i = pl.program_id(0)
    e = be_ref[i]
    used = e < N_EXPERTS
    fresh = jnp.logical_or(i == 0, e != be_ref[jnp.maximum(i - 1, 0)])

    @pl.when(jnp.logical_and(used, fresh))
    def _():
        w13_s[:, :D_EXPERT] = w1_ref[0].astype(BF16)
        w13_s[:, D_EXPERT:] = w3_ref[0].astype(BF16)
        w2_s[...] = w2_ref[0].astype(BF16)

    @pl.when(used)
    def _():
        h13 = _dot(_load_token_rows(x_ref, FFN_BLOCK).astype(BF16), w13_s[...])
        a = h13[:, :D_EXPERT]
        act = (a * jax.nn.sigmoid(a) * h13[:, D_EXPERT:]).astype(BF16)
        _store_token_rows(y_ref, _dot(act, w2_s[...]))

    @pl.when(jnp.logical_not(used))
    def _():
        y_ref[...] = jnp.zeros_like(y_ref)


def _ffn(blk_e, buf, w1, w3, w2):
    nb = buf.shape[0] // (FFN_BLOCK * ROW_TILE)
    d = w1.shape[1]
    emap = lambda i, be: (jnp.minimum(be[i], N_EXPERTS - 1), 0, 0)
    rows = pl.BlockSpec((FFN_BLOCK * ROW_TILE, LANES), lambda i, be: (i, 0))
    return pl.pallas_call(
        _ffn_kernel,
        grid_spec=pltpu.PrefetchScalarGridSpec(
            num_scalar_prefetch=1,
            grid=(nb,),
            in_specs=[rows,
                      pl.BlockSpec((1, d, D_EXPERT), emap),
                      pl.BlockSpec((1, d, D_EXPERT), emap),
                      pl.BlockSpec((1, D_EXPERT, d), emap)],
            out_specs=rows,
            scratch_shapes=[pltpu.VMEM((d, 2 * D_EXPERT), BF16),
                            pltpu.VMEM((D_EXPERT, d), BF16)]),
        out_shape=jax.ShapeDtypeStruct(buf.shape, F32),
        compiler_params=pltpu.CompilerParams(
            dimension_semantics=("arbitrary",), vmem_limit_bytes=VMEM_LIMIT),
        name="ffn",
    )(blk_e, buf, w1, w3, w2)


def _final_kernel(dest_ref, dnext_ref, x1_ref, p_ref, gt_ref, ybuf_hbm, gple_ref, wpg_ref,
                  wple_ref, gfin_ref, out_ref, y_ref, sems):
    i = pl.program_id(0)
    r = x1_ref.shape[0]
    slot = i % 2

    def issue_group(d_ref, s, t0):
        idx = [[d_ref[k * r + t0 + u] for k in range(2)] for u in range(DMA_UNROLL)]
        for u in range(DMA_UNROLL):
            for k in range(2):
                _token_copy(ybuf_hbm, idx[u][k], y_ref.at[s, k], t0 + u,
                            sems.at[s]).start(priority=k)

    def drain(s):
        def body(t, carry):
            _token_copy(ybuf_hbm, 0, y_ref.at[s, 0], 0, sems.at[s]).wait()
            return carry

        lax.fori_loop(0, 2 * r, body, 0, unroll=DMA_UNROLL)

    @pl.when(i == 0)
    def _():
        def body(g, carry):
            issue_group(dest_ref, 0, g * DMA_UNROLL)
            return carry

        lax.fori_loop(0, r // DMA_UNROLL, body, 0)

    drain(slot)

    n_batch = 4
    per = r // n_batch

    def issue_batch(b):
        for t0 in range(b * per, (b + 1) * per, DMA_UNROLL):
            issue_group(dnext_ref, 1 - slot, t0)

    issue_batch(0)
    x2 = (x1_ref[...] + gt_ref[:, 0:1] * _load_token_rows(y_ref.at[slot, 0], r)
          + gt_ref[:, 1:2] * _load_token_rows(y_ref.at[slot, 1], r))
    h3 = _rms(x2, gple_ref[...]).astype(BF16)
    issue_batch(1)
    gate = jax.nn.sigmoid(_dot(h3, wpg_ref[...]))
    issue_batch(2)
    pe = _dot(p_ref[...].astype(BF16), wple_ref[...])
    issue_batch(3)
    x3 = x2 + gate * pe
    out_ref[...] = _rms(x3, gfin_ref[...])

    @pl.when(i + 1 == pl.num_programs(0))
    def _():
        drain(1 - slot)


def _final(dest, r, x1, p, gt, ybuf, g_ple, w_pg, w_ple, g_final):
    t, d = x1.shape
    nt = t // r

    def full(a):
        nd = a.ndim
        return pl.BlockSpec(a.shape, lambda i, _nd=nd: (0,) * _nd, pipeline_mode=pl.Buffered(1))

    return pl.pallas_call(
        _final_kernel,
        grid=(nt,),
        in_specs=[pl.BlockSpec((2 * r,), lambda i: (i,), memory_space=pltpu.SMEM),
                  pl.BlockSpec((2 * r,), lambda i: (jnp.minimum(i + 1, nt - 1),),
                               memory_space=pltpu.SMEM),
                  pl.BlockSpec((r, d), lambda i: (i, 0)),
                  pl.BlockSpec((r, p.shape[1]), lambda i: (i, 0)),
                  pl.BlockSpec((r, ROUTE_COLS), lambda i: (i, 0)),
                  pl.BlockSpec(memory_space=pl.ANY),
                  full(g_ple), full(w_pg), full(w_ple), full(g_final)],
        out_specs=pl.BlockSpec((r, d), lambda i: (i, 0)),
        out_shape=jax.ShapeDtypeStruct((t, d), F32),
        scratch_shapes=[pltpu.VMEM((2, 2, r * ROW_TILE, LANES), F32),
                        pltpu.SemaphoreType.DMA((2,))],
        compiler_params=pltpu.CompilerParams(
            dimension_semantics=("arbitrary",), vmem_limit_bytes=VMEM_LIMIT),
        name="final",
    )(dest, dest, x1, p, gt, ybuf, g_ple, w_pg, w_ple, g_final)


def kernel(x, p, g_mix, w_in, gm_ln_g, gm_ln_b, gm_w_sp, gm_b_sp, w_up_a, hg_lb_param, hg_norm_g,
           w_up_b, w_out, g_ffn, w_grp, w_exp, w1, w3, w2, g_ple, w_pg, w_ple, g_final):
    bsz, seq, d = x.shape
    t = bsz * seq
    row = lambda a: a.reshape(1, -1)

    assert w_in.shape[0] == 1 and hg_lb_param.shape[0] == 2, "single-layer block"
    i = 0
    w_rt = jnp.zeros((d, ROUTE_COLS), F32)
    w_rt = w_rt.at[:, 0:N_GROUPS].set(w_grp[i])
    w_rt = w_rt.at[:, EXP_ROW0:EXP_ROW0 + N_EXPERTS].set(w_exp[i])
    w_rt_hi = w_rt.astype(BF16)
    w_rt_lo = (w_rt - w_rt_hi.astype(F32)).astype(BF16)
    w_rt2 = jnp.concatenate([w_rt_hi, w_rt_lo], axis=1)
    bfull = jnp.repeat(gm_b_sp[i].T, GM_WIDTH // GM_GROUPS, axis=1)

    x1, h2, logits = _mixer(
        x, row(g_mix[i]), w_in[i].astype(BF16), row(gm_ln_g[i]), row(gm_ln_b[i]), gm_w_sp[i],
        bfull, w_up_a[i].astype(BF16), hg_lb_param, row(hg_norm_g[i]),
        w_up_b[i].astype(BF16), w_out[i].astype(BF16), row(g_ffn[i]), w_rt2, w_rt_hi)

    n_rows = 2 * t + N_EXPERTS * FFN_BLOCK
    nb = n_rows // FFN_BLOCK
    nbp = -(-nb // LANES) * LANES
    dest, gt, binfo = _route(logits.reshape(t, ROUTE_COLS), nbp)
    r = dest.shape[2]
    dest = dest[:, 0:2, :].reshape(-1)
    buf = _scatter(dest, r, h2, n_rows)
    ybuf = _ffn(binfo[0, :nb], buf, w1[i], w3[i], w2[i])
    out = _final(dest, r, x1.reshape(t, d), p[i].reshape(t, -1), gt, ybuf, row(g_ple[i]),
                 w_pg[i].astype(BF16), w_ple[i].astype(BF16), row(g_final))
    return out.reshape(bsz, seq, d)
```

```python
import functools

import jax
import jax.numpy as jnp
from jax import lax
from jax.experimental import pallas as pl
from jax.experimental.pallas import tpu as pltpu

F32 = jnp.float32
BF16 = jnp.bfloat16

EPS = 1e-6
GM_WIDTH = 512
GM_GROUPS = 8
GM_CHUNK = 128
HG_HEADS = 4
HG_KEY = 128
HG_CHUNK = 64
HG_KW = HG_HEADS * HG_KEY
N_GROUPS = 4
EXP_PER_GROUP = 8
N_EXPERTS = N_GROUPS * EXP_PER_GROUP
D_EXPERT = 512
LANES = 128
ROUTE_COLS = LANES
EXP_ROW0 = 8

MIX_TILE = 512
ROUTE_TILE = 512
ROUTE_SUBTILES = 4
FFN_BLOCK = 512
VMEM_LIMIT = 56 * 1024 * 1024


def _rms(x, g):
    return x * lax.rsqrt(jnp.mean(x * x, axis=-1, keepdims=True) + EPS) * g


def _dot(a, b):
    return jnp.dot(a, b, preferred_element_type=F32)


def _dot_nt(a, b):
    return lax.dot_general(a, b, (((1,), (1,)), ((), ())), preferred_element_type=F32)


def _dot_tn(a, b):
    return lax.dot_general(a, b, (((0,), (0,)), ((), ())), preferred_element_type=F32)


ROW_TILE = 8


def _store_token_rows(ref, val):
    n = val.shape[0]
    for c in range(ROW_TILE):
        ref[pl.ds(c, n, stride=ROW_TILE), :] = val[:, c * LANES:(c + 1) * LANES]


def _load_token_rows(ref, n, first=0):
    return jnp.concatenate(
        [ref[pl.ds(first * ROW_TILE + c, n, stride=ROW_TILE), :] for c in range(ROW_TILE)], axis=1)


def _mixer_kernel(x_ref, gmix_ref, win_ref, lng_ref, lnb_ref, wsp_ref, bfull_ref, wupa_ref,
                  lbp_ref, ng_ref, wupb_ref, wout_ref, gffn_ref, wrt2_ref, wrt1_ref,
                  x1_ref, h2_ref, lg_ref,
                  z_ref, h_ref, ya_ref, yb_ref, m_ref, st_ref, qd_ref, oi_ref, ds_ref, sc_ref,
                  dec_ref, b_ref, ki_ref, ke_ref, att_ref):
    ts = x_ref.shape[1]
    in_cols = win_ref.shape[1]

    @pl.when(pl.program_id(1) == 0)
    def _():
        st_ref[...] = jnp.zeros_like(st_ref)

    h_ref[...] = _rms(x_ref[0], gmix_ref[...]).astype(BF16)
    def in_proj(col0):
        for n in range(col0, col0 + 1024, 512):
            z_ref[:, n:n + 512] = _dot(h_ref[...], win_ref[:, n:n + 512])

    row = lax.broadcasted_iota(jnp.int32, (GM_CHUNK, GM_CHUNK), 0)
    col = lax.broadcasted_iota(jnp.int32, (GM_CHUNK, GM_CHUNK), 1)
    tril = row >= col
    low_half = col < (LANES // 2)
    ucols = slice(0, GM_WIDTH)
    ncols = slice(GM_WIDTH, 2 * GM_WIDTH)

    def gm_norm(c):
        rows = pl.ds(c * GM_CHUNK, GM_CHUNK)
        z_ref[rows, ucols] = jax.nn.gelu(z_ref[rows, ucols])
        v = jax.nn.gelu(z_ref[rows, ncols])
        d = v - jnp.mean(v, axis=-1, keepdims=True)
        var = jnp.mean(d * d, axis=-1, keepdims=True)
        z_ref[rows, ncols] = d * lax.rsqrt(var + EPS) * lng_ref[...] + lnb_ref[...]

    def gm_gate(c):
        rows = pl.ds(c * GM_CHUNK, GM_CHUNK)
        parts = []
        for p in range(GM_GROUPS // 2):
            vp = z_ref[rows, GM_WIDTH + p * LANES:GM_WIDTH + (p + 1) * LANES]
            v_lo = jnp.where(low_half, vp, 0.0).astype(BF16)
            v_hi = jnp.where(low_half, 0.0, vp).astype(BF16)
            w_lo = jnp.where(tril, wsp_ref[2 * p], 0.0).astype(BF16)
            w_hi = jnp.where(tril, wsp_ref[2 * p + 1], 0.0).astype(BF16)
            parts.append(_dot(w_lo, v_lo) + _dot(w_hi, v_hi))
        s = jnp.concatenate(parts, axis=1) + bfull_ref[...]
        ya_ref[rows, :] = (z_ref[rows, ucols] * s).astype(BF16)

    lbp = lbp_ref[...]
    lmax = jnp.maximum(lbp[0:1], lbp[1:2])
    e0 = jnp.exp(lbp[0:1] - lmax)
    e1 = jnp.exp(lbp[1:2] - lmax)
    lb = e0 / (e0 + e1)

    blk = 2 * HG_CHUNK
    brow = lax.broadcasted_iota(jnp.int32, (blk, blk), 0)
    bcol = lax.broadcasted_iota(jnp.int32, (blk, blk), 1)
    causal = jnp.logical_and(brow >= bcol, (brow < HG_CHUNK) == (bcol < HG_CHUNK))
    tri = jnp.where(causal, 1.0, 0.0).astype(BF16)
    first_chunk = lax.broadcasted_iota(jnp.int32, (blk, HG_KW), 0) < HG_CHUNK
    q0 = 2 * GM_WIDTH
    n_chunks = ts // HG_CHUNK

    n_rb = ts // blk
    qcols = slice(q0, q0 + HG_KW)
    fcols = slice(q0 + HG_KW, q0 + 2 * HG_KW)
    vcols = slice(q0 + 2 * HG_KW, q0 + 3 * HG_KW)
    heads = [slice(hh * HG_KEY, (hh + 1) * HG_KEY) for hh in range(HG_HEADS)]

    def hg_cumsum(rb):
        rows = pl.ds(rb * blk, blk)
        zq = z_ref[rows, qcols]
        f = lb + (1.0 - lb) * jax.nn.sigmoid(z_ref[rows, fcols])
        logf = jnp.log(f)
        lhi = logf.astype(BF16)
        llo = (logf - lhi.astype(F32)).astype(BF16)
        b_ref[rows, :] = _dot(tri, lhi) + _dot(tri, llo)
        z_ref[rows, qcols] = zq * jax.nn.sigmoid(zq)
        z_ref[rows, fcols] = 1.0 - f

    def hg_decay(rb):
        rows = pl.ds(rb * blk, blk)
        b = b_ref[rows, :]
        qf = z_ref[rows, qcols]
        k = z_ref[rows, fcols]
        b_mid = b[HG_CHUNK - 1:HG_CHUNK, :]
        b_end = b[blk - 1:blk, :]
        b_last = jnp.where(first_chunk, b_mid, b_end)
        qd_ref[rows, :] = (qf * jnp.exp(b)).astype(BF16)
        ki_ref[rows, :] = (k * jnp.exp(-b)).astype(BF16)
        ke_ref[rows, :] = (k * jnp.exp(b_last - b)).astype(BF16)
        dec_ref[2 * rb:2 * rb + 1, :] = jnp.exp(b_mid)
        dec_ref[2 * rb + 1:2 * rb + 2, :] = jnp.exp(b_end)

    def hg_scores(rb):
        rows = pl.ds(rb * blk, blk)
        for hh, sl in enumerate(heads):
            att = _dot_nt(qd_ref[rows, sl], ki_ref[rows, sl])
            att_ref[rb, hh] = jnp.where(causal, att, 0.0).astype(BF16)

    def hg_values(rb):
        rows = pl.ds(rb * blk, blk)
        vb = z_ref[rows, vcols].astype(BF16)
        for hh, sl in enumerate(heads):
            oi_ref[rows, sl] = _dot(att_ref[rb, hh], vb[:, sl])
        for cc in range(2):
            crows = pl.ds(rb * blk + cc * HG_CHUNK, HG_CHUNK)
            for hh, sl in enumerate(heads):
                ds_ref[2 * rb + cc, hh] = _dot_tn(
                    vb[cc * HG_CHUNK:(cc + 1) * HG_CHUNK, sl], ke_ref[crows, sl])

    def hg_scan(hh):
        sl = slice(hh * HG_KEY, (hh + 1) * HG_KEY)
        st = st_ref[hh]
        for c in range(n_chunks):
            sc_ref[c, hh] = st.astype(BF16)
            st = st * dec_ref[c:c + 1, sl] + ds_ref[c, hh]
        st_ref[hh] = st

    def hg_out(c):
        rows = pl.ds(c * HG_CHUNK, HG_CHUNK)
        og = jax.nn.sigmoid(z_ref[rows, q0 + 3 * HG_KW:q0 + 4 * HG_KW])
        outs = []
        for hh in range(HG_HEADS):
            sl = slice(hh * HG_KEY, (hh + 1) * HG_KEY)
            o = oi_ref[rows, sl] + _dot_nt(qd_ref[rows, sl], sc_ref[c, hh])
            o = o * lax.rsqrt(jnp.mean(o * o, axis=-1, keepdims=True) + EPS) * ng_ref[:, sl]
            outs.append(o * og[:, sl])
        yb_ref[rows, :] = jnp.concatenate(outs, axis=1).astype(BF16)

    n_gm = ts // GM_CHUNK
    in_proj(q0)
    in_proj(q0 + 2 * HG_KW)
    for rb in range(n_rb):
        hg_cumsum(rb)
    in_proj(0)
    for rb in range(n_rb):
        hg_decay(rb)
    for rb in range(n_rb):
        hg_scores(rb)
    in_proj(q0 + 4 * HG_KW)
    for c in range(n_gm):
        gm_norm(c)
    for rb in range(n_rb):
        hg_values(rb)
    in_proj(q0 + 4 * HG_KW + 1024)
    for c in range(n_gm):
        gm_gate(c)
    for hh in range(HG_HEADS):
        hg_scan(hh)
    for c in range(n_chunks):
        hg_out(c)

    ga0 = q0 + 4 * HG_KW
    gb0 = ga0 + x_ref.shape[2]
    for n in range(0, x_ref.shape[2], 512):
        ua = _dot(ya_ref[...], wupa_ref[:, n:n + 512])
        ub = _dot(yb_ref[...], wupb_ref[:, n:n + 512])
        merged = (jax.nn.sigmoid(z_ref[:, ga0 + n:ga0 + n + 512]) * ua
                  + jax.nn.sigmoid(z_ref[:, gb0 + n:gb0 + n + 512]) * ub)
        m_ref[:, n:n + 512] = merged.astype(BF16)
    x1 = x_ref[0] + _dot(m_ref[...], wout_ref[...])
    x1_ref[0] = x1
    h2 = _rms(x1, gffn_ref[...])
    _store_token_rows(h2_ref, h2)
    hi = h2.astype(BF16)
    lo = (h2 - hi.astype(F32)).astype(BF16)
    l2 = _dot(hi, wrt2_ref[...])
    lg_ref[0] = l2[:, :ROUTE_COLS] + l2[:, ROUTE_COLS:] + _dot(lo, wrt1_ref[...])


def _mixer(x, g_mix, w_in, ln_g, ln_b, w_sp, bfull, w_up_a, lbp, norm_g, w_up_b, w_out, g_ffn,
           w_rt2, w_rt1):
    bsz, seq, d = x.shape
    ts = min(MIX_TILE, seq)
    in_cols = w_in.shape[1]

    def full(a):
        nd = a.ndim
        return pl.BlockSpec(a.shape, lambda b, j, _nd=nd: (0,) * _nd, pipeline_mode=pl.Buffered(1))

    consts = (g_mix, w_in, ln_g, ln_b, w_sp, bfull, w_up_a, lbp, norm_g, w_up_b, w_out, g_ffn,
              w_rt2, w_rt1)
    assert d == ROW_TILE * LANES
    nj = seq // ts
    tile = lambda w: pl.BlockSpec((1, ts, w), lambda b, j: (b, j, 0))
    return pl.pallas_call(
        _mixer_kernel,
        grid=(bsz, nj),
        in_specs=[tile(d)] + [full(a) for a in consts],
        out_specs=[tile(d),
                   pl.BlockSpec((ts * ROW_TILE, LANES), lambda b, j: (b * nj + j, 0)),
                   tile(ROUTE_COLS)],
        out_shape=[jax.ShapeDtypeStruct((bsz, seq, d), F32),
                   jax.ShapeDtypeStruct((bsz * seq * ROW_TILE, LANES), F32),
                   jax.ShapeDtypeStruct((bsz, seq, ROUTE_COLS), F32)],
        scratch_shapes=[pltpu.VMEM((ts, in_cols), F32),
                        pltpu.VMEM((ts, d), BF16),
                        pltpu.VMEM((ts, GM_WIDTH), BF16),
                        pltpu.VMEM((ts, HG_KW), BF16),
                        pltpu.VMEM((ts, d), BF16),
                        pltpu.VMEM((HG_HEADS, HG_KEY, HG_KEY), F32),
                        pltpu.VMEM((ts, HG_KW), BF16),
                        pltpu.VMEM((ts, HG_KW), F32),
                        pltpu.VMEM((ts // HG_CHUNK, HG_HEADS, HG_KEY, HG_KEY), F32),
                        pltpu.VMEM((ts // HG_CHUNK, HG_HEADS, HG_KEY, HG_KEY), BF16),
                        pltpu.VMEM((ts // HG_CHUNK, HG_KW), F32),
                        pltpu.VMEM((ts, HG_KW), F32),
                        pltpu.VMEM((ts, HG_KW), BF16),
                        pltpu.VMEM((ts, HG_KW), BF16),
                        pltpu.VMEM((ts // (2 * HG_CHUNK), HG_HEADS, 2 * HG_CHUNK, 2 * HG_CHUNK),
                                   BF16)],
        compiler_params=pltpu.CompilerParams(
            dimension_semantics=("arbitrary", "arbitrary"), vmem_limit_bytes=VMEM_LIMIT),
        name="mixer",
    )(x, *consts)


def _route_kernel(lg_ref, utri_ref, dest_ref, gt_ref, binfo_ref, einfo_ref, rec_ref, cnt_ref):
    phase = pl.program_id(0)
    i = pl.program_id(1)
    r = utri_ref.shape[0]
    n_sub = lg_ref.shape[0] // r
    sub8 = lax.broadcasted_iota(jnp.int32, (8, r), 0)
    e32 = lax.broadcasted_iota(jnp.int32, (N_EXPERTS, r), 0).astype(F32)

    @pl.when(jnp.logical_and(phase == 0, i == 0))
    def _():
        cnt_ref[...] = jnp.zeros_like(cnt_ref)

    def rank_tile(s):
        lt = lg_ref[s * r:(s + 1) * r, :].T
        l0, l1, l2, l3 = lt[0:1], lt[1:2], lt[2:3], lt[3:4]
        gmax = jnp.maximum(jnp.maximum(l0, l1), jnp.maximum(l2, l3))
        gsum = (jnp.exp(l0 - gmax) + jnp.exp(l1 - gmax)) + (jnp.exp(l2 - gmax) + jnp.exp(l3 - gmax))
        p_g = 1.0 / gsum
        gsel = jnp.where(l0 == gmax, 0, jnp.where(l1 == gmax, 1, jnp.where(l2 == gmax, 2, 3)))
        eg = [lt[EXP_ROW0 + EXP_PER_GROUP * g:EXP_ROW0 + EXP_PER_GROUP * (g + 1)]
              for g in range(N_GROUPS)]
        sel = jnp.where(gsel == 0, eg[0], jnp.where(gsel == 1, eg[1],
                                                     jnp.where(gsel == 2, eg[2], eg[3])))
        m1 = jnp.max(sel, axis=0, keepdims=True)
        i1 = jnp.min(jnp.where(sel == m1, sub8, EXP_PER_GROUP), axis=0, keepdims=True)
        sel2 = jnp.where(sub8 == i1, -jnp.inf, sel)
        m2 = jnp.max(sel2, axis=0, keepdims=True)
        i2 = jnp.min(jnp.where(sel2 == m2, sub8, EXP_PER_GROUP), axis=0, keepdims=True)
        ex = jnp.exp(m2 - m1)
        g1 = 1.0 / (1.0 + ex)
        g2 = ex * g1
        eid1 = (gsel * EXP_PER_GROUP + i1).astype(F32)
        eid2 = (gsel * EXP_PER_GROUP + i2).astype(F32)
        oh1 = jnp.where(e32 == eid1, 1.0, 0.0)
        oh2 = jnp.where(e32 == eid2, 1.0, 0.0)
        cum1 = _dot(oh1.astype(BF16), utri_ref[...])
        cum2 = _dot(oh2.astype(BF16), utri_ref[...])
        tot1 = jnp.sum(oh1, axis=1, keepdims=True)
        tot2 = jnp.sum(oh2, axis=1, keepdims=True)
        base = cnt_ref[:, 0:1]
        rank1 = jnp.sum(oh1 * (base + cum1), axis=0, keepdims=True)
        rank2 = jnp.sum(oh2 * (base + tot1 + cum2), axis=0, keepdims=True)
        cnt_ref[...] = cnt_ref[...] + (tot1 + tot2)
        rec = jnp.where(sub8 == 0, eid1, jnp.where(sub8 == 1, eid2, jnp.where(
            sub8 == 2, rank1, jnp.where(sub8 == 3, rank2, jnp.where(
                sub8 == 4, g1 * p_g, jnp.where(sub8 == 5, g2 * p_g, 0.0))))))
        rec_ref[i * n_sub + s] = rec

    @pl.when(phase == 0)
    def _():
        for s in range(n_sub):
            rank_tile(s)

    @pl.when(phase == 1)
    def _():
        cnt = cnt_ref[...]
        padded = jnp.floor((cnt + (FFN_BLOCK - 1)) * (1.0 / FFN_BLOCK)) * FFN_BLOCK
        rows = lax.broadcasted_iota(jnp.int32, cnt.shape, 0)
        pad_end = padded
        for s in (1, 2, 4, 8, 16):
            pad_end = pad_end + jnp.where(rows >= s, pltpu.roll(pad_end, s, axis=0), 0.0)
        pad_start = (pad_end - padded)[:, 0:1]
        subw = lax.broadcasted_iota(jnp.int32, (ROUTE_COLS, r), 0)
        for s in range(n_sub):
            rec = rec_ref[i * n_sub + s]
            d1 = rec[2:3] + jnp.sum(jnp.where(e32 == rec[0:1], pad_start, 0.0), axis=0,
                                    keepdims=True)
            d2 = rec[3:4] + jnp.sum(jnp.where(e32 == rec[1:2], pad_start, 0.0), axis=0,
                                    keepdims=True)
            dest_ref[s] = jnp.where(sub8 == 0, d1, jnp.where(sub8 == 1, d2, 0.0)).astype(jnp.int32)
            gates = jnp.where(subw == 0, rec[4:5], jnp.where(subw == 1, rec[5:6], 0.0))
            gt_ref[s * r:(s + 1) * r, :] = gates.T
        nbp = binfo_ref.shape[1]
        blk_start = (lax.broadcasted_iota(jnp.int32, (N_EXPERTS, nbp), 1) * FFN_BLOCK).astype(F32)
        n_le = jnp.sum(jnp.where(pad_end[:, 0:1] <= blk_start, 1.0, 0.0), axis=0, keepdims=True)
        binfo_ref[...] = jnp.broadcast_to(n_le, binfo_ref.shape).astype(jnp.int32)
        lane = lax.broadcasted_iota(jnp.int32, cnt.shape, 1)
        einfo_ref[...] = jnp.where(lane == 0, pad_end - padded + cnt, pad_end).astype(jnp.int32)


def _route(logits, nbp):
    t = logits.shape[0]
    r = min(ROUTE_TILE, t)
    rr = min(ROUTE_SUBTILES * r, t)
    nt = t // rr
    utri = jnp.triu(jnp.ones((r, r), F32), k=1).astype(BF16)
    return pl.pallas_call(
        _route_kernel,
        grid=(2, nt),
        in_specs=[pl.BlockSpec((rr, ROUTE_COLS), lambda p, i: (i * (1 - p) + (nt - 1) * p, 0)),
                  pl.BlockSpec((r, r), lambda p, i: (0, 0))],
        out_specs=[pl.BlockSpec((rr // r, 8, r), lambda p, i: (i * p, 0, 0)),
                   pl.BlockSpec((rr, ROUTE_COLS), lambda p, i: (i * p, 0)),
                   pl.BlockSpec((8, nbp), lambda p, i: (0, 0)),
                   pl.BlockSpec((N_EXPERTS, LANES), lambda p, i: (0, 0))],
        out_shape=[jax.ShapeDtypeStruct((t // r, 8, r), jnp.int32),
                   jax.ShapeDtypeStruct((t, ROUTE_COLS), F32),
                   jax.ShapeDtypeStruct((8, nbp), jnp.int32),
                   jax.ShapeDtypeStruct((N_EXPERTS, LANES), jnp.int32)],
        scratch_shapes=[pltpu.VMEM((t // r, 8, r), F32),
                        pltpu.VMEM((N_EXPERTS, LANES), F32)],
        compiler_params=pltpu.CompilerParams(
            dimension_semantics=("arbitrary", "arbitrary"), vmem_limit_bytes=VMEM_LIMIT),
        name="route",
    )(logits, utri)


DMA_UNROLL = 8


def _token_copy(src, src_tok, dst, dst_tok, sem):
    return pltpu.make_async_copy(
        src.at[pl.ds(pl.multiple_of(src_tok * ROW_TILE, ROW_TILE), ROW_TILE)],
        dst.at[pl.ds(pl.multiple_of(dst_tok * ROW_TILE, ROW_TILE), ROW_TILE)], sem)


def _pad_copies(plo_ref, phi_ref, zero_ref, buf_hbm, sem):
    copies = []
    for e in range(N_EXPERTS):
        lo = plo_ref[e]
        n = phi_ref[e] - lo
        bit = FFN_BLOCK // 2
        while bit >= 1:
            first = lo + (n & ~(2 * bit - 1))
            copies.append(((n & bit) != 0, pltpu.make_async_copy(
                zero_ref.at[pl.ds(0, bit * ROW_TILE)],
                buf_hbm.at[pl.ds(pl.multiple_of(first * ROW_TILE, ROW_TILE), bit * ROW_TILE)],
                sem)))
            bit //= 2
    blk_rows = FFN_BLOCK * ROW_TILE
    n_blocks = buf_hbm.shape[0] // blk_rows
    n_used = phi_ref[N_EXPERTS - 1] // FFN_BLOCK
    for j in range(N_EXPERTS):
        for half in range(2):
            row0 = pl.multiple_of((n_used + j) * blk_rows + half * (blk_rows // 2), ROW_TILE)
            copies.append((n_used + j < n_blocks, pltpu.make_async_copy(
                zero_ref, buf_hbm.at[pl.ds(row0, blk_rows // 2)], sem)))
    return copies


def _scatter_kernel(plo_ref, phi_ref, dest_ref, h2_ref, buf_hbm, zero_ref, sem, zsem):
    r = dest_ref.shape[0] // 2

    @pl.when(pl.program_id(0) == 0)
    def _():
        zero_ref[...] = jnp.zeros_like(zero_ref)
        for cond, cp in _pad_copies(plo_ref, phi_ref, zero_ref, buf_hbm, zsem):
            pl.when(cond)(cp.start)
        for cond, cp in _pad_copies(plo_ref, phi_ref, zero_ref, buf_hbm, zsem):
            pl.when(cond)(cp.wait)

    def issue(g, carry):
        t0 = g * DMA_UNROLL
        idx = [[dest_ref[k * r + t0 + u] for k in range(2)] for u in range(DMA_UNROLL)]
        for u in range(DMA_UNROLL):
            for k in range(2):
                _token_copy(h2_ref, t0 + u, buf_hbm, idx[u][k], sem).start(priority=k)
        return carry

    lax.fori_loop(0, r // DMA_UNROLL, issue, 0)

    def drain(t, carry):
        _token_copy(h2_ref, 0, buf_hbm, 0, sem).wait()
        return carry

    lax.fori_loop(0, 2 * r, drain, 0, unroll=DMA_UNROLL)


def _scatter(pad_lo, pad_hi, dest, r, h2, n_rows):
    nt = dest.shape[0] // (2 * r)
    return pl.pallas_call(
        _scatter_kernel,
        grid_spec=pltpu.PrefetchScalarGridSpec(
            num_scalar_prefetch=2,
            grid=(nt,),
            in_specs=[pl.BlockSpec((2 * r,), lambda i, lo, hi: (i,), memory_space=pltpu.SMEM),
                      pl.BlockSpec((r * ROW_TILE, LANES), lambda i, lo, hi: (i, 0))],
            out_specs=pl.BlockSpec(memory_space=pl.ANY),
            scratch_shapes=[pltpu.VMEM((FFN_BLOCK // 2 * ROW_TILE, LANES), F32),
                            pltpu.SemaphoreType.DMA(()),
                            pltpu.SemaphoreType.DMA(())]),
        out_shape=jax.ShapeDtypeStruct((n_rows * ROW_TILE, LANES), F32),
        compiler_params=pltpu.CompilerParams(
            dimension_semantics=("arbitrary",), has_side_effects=True),
        name="scatter",
    )(pad_lo, pad_hi, dest, h2)


def _ffn_kernel(be_ref, nu_ref, x_ref, w1_ref, w3_ref, w2_ref, y_ref, w13_s, w2_s):
    del nu_ref
    i = pl.program_id(0)
    e = be_ref[i]
    used = e < N_EXPERTS
    fresh = jnp.logical_or(i == 0, e != be_ref[jnp.maximum(i - 1, 0)])

    @pl.when(jnp.logical_and(used, fresh))
    def _():
        w13_s[:, :D_EXPERT] = w1_ref[0].astype(BF16)
        w13_s[:, D_EXPERT:] = w3_ref[0].astype(BF16)
        w2_s[...] = w2_ref[0].astype(BF16)

    @pl.when(used)
    def _():
        h13 = _dot(_load_token_rows(x_ref, FFN_BLOCK).astype(BF16), w13_s[...])
        a = h13[:, :D_EXPERT]
        act = (a * jax.nn.sigmoid(a) * h13[:, D_EXPERT:]).astype(BF16)
        _store_token_rows(y_ref, _dot(act, w2_s[...]))

    @pl.when(jnp.logical_not(used))
    def _():
        y_ref[...] = jnp.zeros_like(y_ref)


def _ffn(blk_e, n_used, buf, w1, w3, w2):
    nb = buf.shape[0] // (FFN_BLOCK * ROW_TILE)
    d = w1.shape[1]
    emap = lambda i, be, nu: (jnp.minimum(be[i], N_EXPERTS - 1), 0, 0)
    rows = pl.BlockSpec((FFN_BLOCK * ROW_TILE, LANES), lambda i, be, nu: (i, 0))
    rows_in = pl.BlockSpec((FFN_BLOCK * ROW_TILE, LANES),
                           lambda i, be, nu: (jnp.minimum(i, nu[0] - 1), 0))
    return pl.pallas_call(
        _ffn_kernel,
        grid_spec=pltpu.PrefetchScalarGridSpec(
            num_scalar_prefetch=2,
            grid=(nb,),
            in_specs=[rows_in,
                      pl.BlockSpec((1, d, D_EXPERT), emap),
                      pl.BlockSpec((1, d, D_EXPERT), emap),
                      pl.BlockSpec((1, D_EXPERT, d), emap)],
            out_specs=rows,
            scratch_shapes=[pltpu.VMEM((d, 2 * D_EXPERT), BF16),
                            pltpu.VMEM((D_EXPERT, d), BF16)]),
        out_shape=jax.ShapeDtypeStruct(buf.shape, F32),
        compiler_params=pltpu.CompilerParams(
            dimension_semantics=("arbitrary",), vmem_limit_bytes=VMEM_LIMIT),
        name="ffn",
    )(blk_e, n_used, buf, w1, w3, w2)


def _final_kernel(dest_ref, dnext_ref, x1_ref, p_ref, gt_ref, ybuf_hbm, gple_ref, wpg_ref,
                  wple_ref, gfin_ref, out_ref, y_ref, x2_ref, h3_ref, g_ref, pb_ref, sems):
    i = pl.program_id(0)
    r = x1_ref.shape[0]
    slot = i % 2

    def issue_group(d_ref, s, t0):
        idx = [[d_ref[k * r + t0 + u] for k in range(2)] for u in range(DMA_UNROLL)]
        for u in range(DMA_UNROLL):
            for k in range(2):
                _token_copy(ybuf_hbm, idx[u][k], y_ref.at[s, k], t0 + u,
                            sems.at[s]).start(priority=k)

    def drain(s):
        def body(t, carry):
            _token_copy(ybuf_hbm, 0, y_ref.at[s, 0], 0, sems.at[s]).wait()
            return carry

        lax.fori_loop(0, 2 * r, body, 0, unroll=DMA_UNROLL)

    @pl.when(i == 0)
    def _():
        def body(g, carry):
            issue_group(dest_ref, 0, g * DMA_UNROLL)
            return carry

        lax.fori_loop(0, r // DMA_UNROLL, body, 0)

    drain(slot)

    n_batch = 4
    per = r // n_batch

    def issue_batch(b):
        for t0 in range(b * per, (b + 1) * per, DMA_UNROLL):
            issue_group(dnext_ref, 1 - slot, t0)

    n_chunk = 8
    rc = r // n_chunk
    issue_batch(0)
    for c in range(n_chunk):
        rows = pl.ds(c * rc, rc)
        x2 = (x1_ref[rows, :]
              + gt_ref[rows, 0:1] * _load_token_rows(y_ref.at[slot, 0], rc, c * rc)
              + gt_ref[rows, 1:2] * _load_token_rows(y_ref.at[slot, 1], rc, c * rc))
        x2_ref[rows, :] = x2
        h3_ref[rows, :] = _rms(x2, gple_ref[...]).astype(BF16)
    issue_batch(1)
    pb_ref[...] = p_ref[...].astype(BF16)
    for n in range(0, x2_ref.shape[1], 256):
        cols = slice(n, n + 256)
        g_ref[:, cols] = (jax.nn.sigmoid(_dot(h3_ref[...], wpg_ref[:, cols]))
                          * _dot(pb_ref[...], wple_ref[:, cols]))
        if n == 256:
            issue_batch(2)
    issue_batch(3)
    for c in range(n_chunk):
        rows = pl.ds(c * rc, rc)
        out_ref[rows, :] = _rms(x2_ref[rows, :] + g_ref[rows, :], gfin_ref[...])

    @pl.when(i + 1 == pl.num_programs(0))
    def _():
        drain(1 - slot)


def _final(dest, r, x1, p, gt, ybuf, g_ple, w_pg, w_ple, g_final):
    t, d = x1.shape
    nt = t // r

    def full(a):
        nd = a.ndim
        return pl.BlockSpec(a.shape, lambda i, _nd=nd: (0,) * _nd, pipeline_mode=pl.Buffered(1))

    return pl.pallas_call(
        _final_kernel,
        grid=(nt,),
        in_specs=[pl.BlockSpec((2 * r,), lambda i: (i,), memory_space=pltpu.SMEM),
                  pl.BlockSpec((2 * r,), lambda i: (jnp.minimum(i + 1, nt - 1),),
                               memory_space=pltpu.SMEM),
                  pl.BlockSpec((r, d), lambda i: (i, 0)),
                  pl.BlockSpec((r, p.shape[1]), lambda i: (i, 0)),
                  pl.BlockSpec((r, ROUTE_COLS), lambda i: (i, 0)),
                  pl.BlockSpec(memory_space=pl.ANY),
                  full(g_ple), full(w_pg), full(w_ple), full(g_final)],
        out_specs=pl.BlockSpec((r, d), lambda i: (i, 0)),
        out_shape=jax.ShapeDtypeStruct((t, d), F32),
        scratch_shapes=[pltpu.VMEM((2, 2, r * ROW_TILE, LANES), F32),
                        pltpu.VMEM((r, d), F32),
                        pltpu.VMEM((r, d), BF16),
                        pltpu.VMEM((r, d), F32),
                        pltpu.VMEM((r, p.shape[1]), BF16),
                        pltpu.SemaphoreType.DMA((2,))],
        compiler_params=pltpu.CompilerParams(
            dimension_semantics=("arbitrary",), vmem_limit_bytes=VMEM_LIMIT),
        name="final",
    )(dest, dest, x1, p, gt, ybuf, g_ple, w_pg, w_ple, g_final)


def kernel(x, p, g_mix, w_in, gm_ln_g, gm_ln_b, gm_w_sp, gm_b_sp, w_up_a, hg_lb_param, hg_norm_g,
           w_up_b, w_out, g_ffn, w_grp, w_exp, w1, w3, w2, g_ple, w_pg, w_ple, g_final):
    bsz, seq, d = x.shape
    t = bsz * seq
    row = lambda a: a.reshape(1, -1)

    assert w_in.shape[0] == 1 and hg_lb_param.shape[0] == 2, "single-layer block"
    i = 0
    w_rt = jnp.zeros((d, ROUTE_COLS), F32)
    w_rt = w_rt.at[:, 0:N_GROUPS].set(w_grp[i])
    w_rt = w_rt.at[:, EXP_ROW0:EXP_ROW0 + N_EXPERTS].set(w_exp[i])
    w_rt_hi = w_rt.astype(BF16)
    w_rt_lo = (w_rt - w_rt_hi.astype(F32)).astype(BF16)
    w_rt2 = jnp.concatenate([w_rt_hi, w_rt_lo], axis=1)
    bfull = jnp.repeat(gm_b_sp[i].T, GM_WIDTH // GM_GROUPS, axis=1)

    x1, h2, logits = _mixer(
        x, row(g_mix[i]), w_in[i].astype(BF16), row(gm_ln_g[i]), row(gm_ln_b[i]), gm_w_sp[i],
        bfull, w_up_a[i].astype(BF16), hg_lb_param, row(hg_norm_g[i]),
        w_up_b[i].astype(BF16), w_out[i].astype(BF16), row(g_ffn[i]), w_rt2, w_rt_hi)

    n_rows = 2 * t + N_EXPERTS * FFN_BLOCK
    nb = n_rows // FFN_BLOCK
    nbp = -(-nb // LANES) * LANES
    dest, gt, binfo, einfo = _route(logits.reshape(t, ROUTE_COLS), nbp)
    r = dest.shape[2]
    dest = dest[:, 0:2, :].reshape(-1)
    buf = _scatter(einfo[:, 0], einfo[:, 1], dest, r, h2, n_rows)
    n_used = einfo[N_EXPERTS - 1:, 1] // FFN_BLOCK
    ybuf = _ffn(binfo[0, :nb], n_used, buf, w1[i], w3[i], w2[i])
    out = _final(dest, r, x1.reshape(t, d), p[i].reshape(t, -1), gt, ybuf, row(g_ple[i]),
                 w_pg[i].astype(BF16), w_ple[i].astype(BF16), row(g_final))
    return out.reshape(bsz, seq, d)
```

```python
import functools

import jax
import jax.numpy as jnp
from jax import lax
from jax.experimental import pallas as pl
from jax.experimental.pallas import tpu as pltpu

F32 = jnp.float32
BF16 = jnp.bfloat16

EPS = 1e-6
GM_WIDTH = 512
GM_GROUPS = 8
GM_CHUNK = 128
HG_HEADS = 4
HG_KEY = 128
HG_CHUNK = 64
HG_KW = HG_HEADS * HG_KEY
N_GROUPS = 4
EXP_PER_GROUP = 8
N_EXPERTS = N_GROUPS * EXP_PER_GROUP
D_EXPERT = 512
LANES = 128
ROUTE_COLS = LANES
EXP_ROW0 = 8

MIX_TILE = 512
ROUTE_TILE = 512
ROUTE_SUBTILES = 4
FFN_BLOCK = 512
VMEM_LIMIT = 56 * 1024 * 1024


def _rms(x, g):
    return x * lax.rsqrt(jnp.mean(x * x, axis=-1, keepdims=True) + EPS) * g


def _dot(a, b):
    return jnp.dot(a, b, preferred_element_type=F32)


def _dot_nt(a, b):
    return lax.dot_general(a, b, (((1,), (1,)), ((), ())), preferred_element_type=F32)


def _dot_tn(a, b):
    return lax.dot_general(a, b, (((0,), (0,)), ((), ())), preferred_element_type=F32)


ROW_TILE = 8


def _store_token_rows(ref, val):
    n = val.shape[0]
    for c in range(ROW_TILE):
        ref[pl.ds(c, n, stride=ROW_TILE), :] = val[:, c * LANES:(c + 1) * LANES]


def _load_token_rows(ref, n, first=0):
    return jnp.concatenate(
        [ref[pl.ds(first * ROW_TILE + c, n, stride=ROW_TILE), :] for c in range(ROW_TILE)], axis=1)


def _mixer_kernel(x_ref, gmix_ref, win_ref, lng_ref, lnb_ref, wsp_ref, bfull_ref, wupa_ref,
                  lbp_ref, ng_ref, wupb_ref, wout_ref, gffn_ref, wrt2_ref, wrt1_ref,
                  x1_ref, h2_ref, lg_ref,
                  z_ref, h_ref, ya_ref, yb_ref, m_ref, st_ref, qd_ref, oi_ref, ds_ref, sc_ref,
                  dec_ref, b_ref, ki_ref, ke_ref, att_ref):
    ts = x_ref.shape[1]
    in_cols = win_ref.shape[1]

    @pl.when(pl.program_id(1) == 0)
    def _():
        st_ref[...] = jnp.zeros_like(st_ref)

    h_ref[...] = _rms(x_ref[0], gmix_ref[...]).astype(BF16)
    def in_proj(col0):
        for n in range(col0, col0 + 1024, 512):
            z_ref[:, n:n + 512] = _dot(h_ref[...], win_ref[:, n:n + 512])

    row = lax.broadcasted_iota(jnp.int32, (GM_CHUNK, GM_CHUNK), 0)
    col = lax.broadcasted_iota(jnp.int32, (GM_CHUNK, GM_CHUNK), 1)
    tril = row >= col
    low_half = col < (LANES // 2)
    ucols = slice(0, GM_WIDTH)
    ncols = slice(GM_WIDTH, 2 * GM_WIDTH)

    def gm_norm(c):
        rows = pl.ds(c * GM_CHUNK, GM_CHUNK)
        z_ref[rows, ucols] = jax.nn.gelu(z_ref[rows, ucols])
        v = jax.nn.gelu(z_ref[rows, ncols])
        d = v - jnp.mean(v, axis=-1, keepdims=True)
        var = jnp.mean(d * d, axis=-1, keepdims=True)
        z_ref[rows, ncols] = d * lax.rsqrt(var + EPS) * lng_ref[...] + lnb_ref[...]

    def gm_gate(c):
        rows = pl.ds(c * GM_CHUNK, GM_CHUNK)
        parts = []
        for p in range(GM_GROUPS // 2):
            vp = z_ref[rows, GM_WIDTH + p * LANES:GM_WIDTH + (p + 1) * LANES]
            v_lo = jnp.where(low_half, vp, 0.0).astype(BF16)
            v_hi = jnp.where(low_half, 0.0, vp).astype(BF16)
            w_lo = jnp.where(tril, wsp_ref[2 * p], 0.0).astype(BF16)
            w_hi = jnp.where(tril, wsp_ref[2 * p + 1], 0.0).astype(BF16)
            parts.append(_dot(w_lo, v_lo) + _dot(w_hi, v_hi))
        s = jnp.concatenate(parts, axis=1) + bfull_ref[...]
        ya_ref[rows, :] = (z_ref[rows, ucols] * s).astype(BF16)

    lbp = lbp_ref[...]
    lmax = jnp.maximum(lbp[0:1], lbp[1:2])
    e0 = jnp.exp(lbp[0:1] - lmax)
    e1 = jnp.exp(lbp[1:2] - lmax)
    lb = e0 / (e0 + e1)

    blk = 2 * HG_CHUNK
    brow = lax.broadcasted_iota(jnp.int32, (blk, blk), 0)
    bcol = lax.broadcasted_iota(jnp.int32, (blk, blk), 1)
    causal = jnp.logical_and(brow >= bcol, (brow < HG_CHUNK) == (bcol < HG_CHUNK))
    tri = jnp.where(causal, 1.0, 0.0).astype(BF16)
    first_chunk = lax.broadcasted_iota(jnp.int32, (blk, HG_KW), 0) < HG_CHUNK
    q0 = 2 * GM_WIDTH
    n_chunks = ts // HG_CHUNK

    n_rb = ts // blk
    qcols = slice(q0, q0 + HG_KW)
    fcols = slice(q0 + HG_KW, q0 + 2 * HG_KW)
    vcols = slice(q0 + 2 * HG_KW, q0 + 3 * HG_KW)
    heads = [slice(hh * HG_KEY, (hh + 1) * HG_KEY) for hh in range(HG_HEADS)]

    def hg_cumsum(rb):
        rows = pl.ds(rb * blk, blk)
        zq = z_ref[rows, qcols]
        f = lb + (1.0 - lb) * jax.nn.sigmoid(z_ref[rows, fcols])
        logf = jnp.log(f)
        lhi = logf.astype(BF16)
        llo = (logf - lhi.astype(F32)).astype(BF16)
        b_ref[rows, :] = _dot(tri, lhi) + _dot(tri, llo)
        z_ref[rows, qcols] = zq * jax.nn.sigmoid(zq)
        z_ref[rows, fcols] = 1.0 - f

    def hg_decay(rb):
        rows = pl.ds(rb * blk, blk)
        b = b_ref[rows, :]
        qf = z_ref[rows, qcols]
        k = z_ref[rows, fcols]
        b_mid = b[HG_CHUNK - 1:HG_CHUNK, :]
        b_end = b[blk - 1:blk, :]
        b_last = jnp.where(first_chunk, b_mid, b_end)
        qd_ref[rows, :] = (qf * jnp.exp(b)).astype(BF16)
        ki_ref[rows, :] = (k * jnp.exp(-b)).astype(BF16)
        ke_ref[rows, :] = (k * jnp.exp(b_last - b)).astype(BF16)
        dec_ref[2 * rb:2 * rb + 1, :] = jnp.exp(b_mid)
        dec_ref[2 * rb + 1:2 * rb + 2, :] = jnp.exp(b_end)

    def hg_scores(rb):
        rows = pl.ds(rb * blk, blk)
        for hh, sl in enumerate(heads):
            att = _dot_nt(qd_ref[rows, sl], ki_ref[rows, sl])
            att_ref[rb, hh] = jnp.where(causal, att, 0.0).astype(BF16)

    def hg_values(rb):
        rows = pl.ds(rb * blk, blk)
        vb = z_ref[rows, vcols].astype(BF16)
        for hh, sl in enumerate(heads):
            oi_ref[rows, sl] = _dot(att_ref[rb, hh], vb[:, sl])
        for cc in range(2):
            crows = pl.ds(rb * blk + cc * HG_CHUNK, HG_CHUNK)
            for hh, sl in enumerate(heads):
                ds_ref[2 * rb + cc, hh] = _dot_tn(
                    vb[cc * HG_CHUNK:(cc + 1) * HG_CHUNK, sl], ke_ref[crows, sl])

    def hg_scan(hh):
        sl = slice(hh * HG_KEY, (hh + 1) * HG_KEY)
        st = st_ref[hh]
        for c in range(n_chunks):
            sc_ref[c, hh] = st.astype(BF16)
            st = st * dec_ref[c:c + 1, sl] + ds_ref[c, hh]
        st_ref[hh] = st

    def hg_out(c):
        rows = pl.ds(c * HG_CHUNK, HG_CHUNK)
        og = jax.nn.sigmoid(z_ref[rows, q0 + 3 * HG_KW:q0 + 4 * HG_KW])
        outs = []
        for hh in range(HG_HEADS):
            sl = slice(hh * HG_KEY, (hh + 1) * HG_KEY)
            o = oi_ref[rows, sl] + _dot_nt(qd_ref[rows, sl], sc_ref[c, hh])
            o = o * lax.rsqrt(jnp.mean(o * o, axis=-1, keepdims=True) + EPS) * ng_ref[:, sl]
            outs.append(o * og[:, sl])
        yb_ref[rows, :] = jnp.concatenate(outs, axis=1).astype(BF16)

    n_gm = ts // GM_CHUNK
    in_proj(q0)
    in_proj(q0 + 2 * HG_KW)
    for rb in range(n_rb):
        hg_cumsum(rb)
    in_proj(0)
    for rb in range(n_rb):
        hg_decay(rb)
    for rb in range(n_rb):
        hg_scores(rb)
    in_proj(q0 + 4 * HG_KW)
    for c in range(n_gm):
        gm_norm(c)
    for rb in range(n_rb):
        hg_values(rb)
    in_proj(q0 + 4 * HG_KW + 1024)
    for c in range(n_gm):
        gm_gate(c)
    for hh in range(HG_HEADS):
        hg_scan(hh)
    for c in range(n_chunks):
        hg_out(c)

    ga0 = q0 + 4 * HG_KW
    gb0 = ga0 + x_ref.shape[2]
    for n in range(0, x_ref.shape[2], 512):
        ua = _dot(ya_ref[...], wupa_ref[:, n:n + 512])
        ub = _dot(yb_ref[...], wupb_ref[:, n:n + 512])
        merged = (jax.nn.sigmoid(z_ref[:, ga0 + n:ga0 + n + 512]) * ua
                  + jax.nn.sigmoid(z_ref[:, gb0 + n:gb0 + n + 512]) * ub)
        m_ref[:, n:n + 512] = merged.astype(BF16)
    x1 = x_ref[0] + _dot(m_ref[...], wout_ref[...])
    x1_ref[0] = x1
    h2 = _rms(x1, gffn_ref[...])
    _store_token_rows(h2_ref, h2)
    hi = h2.astype(BF16)
    lo = (h2 - hi.astype(F32)).astype(BF16)
    l2 = _dot(hi, wrt2_ref[...])
    lg_ref[0] = l2[:, :ROUTE_COLS] + l2[:, ROUTE_COLS:] + _dot(lo, wrt1_ref[...])


def _mixer(x, g_mix, w_in, ln_g, ln_b, w_sp, bfull, w_up_a, lbp, norm_g, w_up_b, w_out, g_ffn,
           w_rt2, w_rt1):
    bsz, seq, d = x.shape
    ts = min(MIX_TILE, seq)
    in_cols = w_in.shape[1]

    def full(a):
        nd = a.ndim
        return pl.BlockSpec(a.shape, lambda b, j, _nd=nd: (0,) * _nd, pipeline_mode=pl.Buffered(1))

    consts = (g_mix, w_in, ln_g, ln_b, w_sp, bfull, w_up_a, lbp, norm_g, w_up_b, w_out, g_ffn,
              w_rt2, w_rt1)
    assert d == ROW_TILE * LANES
    nj = seq // ts
    tile = lambda w: pl.BlockSpec((1, ts, w), lambda b, j: (b, j, 0))
    return pl.pallas_call(
        _mixer_kernel,
        grid=(bsz, nj),
        in_specs=[tile(d)] + [full(a) for a in consts],
        out_specs=[tile(d),
                   pl.BlockSpec((ts * ROW_TILE, LANES), lambda b, j: (b * nj + j, 0)),
                   tile(ROUTE_COLS)],
        out_shape=[jax.ShapeDtypeStruct((bsz, seq, d), F32),
                   jax.ShapeDtypeStruct((bsz * seq * ROW_TILE, LANES), F32),
                   jax.ShapeDtypeStruct((bsz, seq, ROUTE_COLS), F32)],
        scratch_shapes=[pltpu.VMEM((ts, in_cols), F32),
                        pltpu.VMEM((ts, d), BF16),
                        pltpu.VMEM((ts, GM_WIDTH), BF16),
                        pltpu.VMEM((ts, HG_KW), BF16),
                        pltpu.VMEM((ts, d), BF16),
                        pltpu.VMEM((HG_HEADS, HG_KEY, HG_KEY), F32),
                        pltpu.VMEM((ts, HG_KW), BF16),
                        pltpu.VMEM((ts, HG_KW), F32),
                        pltpu.VMEM((ts // HG_CHUNK, HG_HEADS, HG_KEY, HG_KEY), F32),
                        pltpu.VMEM((ts // HG_CHUNK, HG_HEADS, HG_KEY, HG_KEY), BF16),
                        pltpu.VMEM((ts // HG_CHUNK, HG_KW), F32),
                        pltpu.VMEM((ts, HG_KW), F32),
                        pltpu.VMEM((ts, HG_KW), BF16),
                        pltpu.VMEM((ts, HG_KW), BF16),
                        pltpu.VMEM((ts // (2 * HG_CHUNK), HG_HEADS, 2 * HG_CHUNK, 2 * HG_CHUNK),
                                   BF16)],
        compiler_params=pltpu.CompilerParams(
            dimension_semantics=("arbitrary", "arbitrary"), vmem_limit_bytes=VMEM_LIMIT),
        name="mixer",
    )(x, *consts)


def _route_kernel(lg_ref, utri_ref, dest_ref, gt_ref, binfo_ref, einfo_ref, rec_ref, cnt_ref):
    phase = pl.program_id(0)
    i = pl.program_id(1)
    r = utri_ref.shape[0]
    n_sub = lg_ref.shape[0] // r
    sub8 = lax.broadcasted_iota(jnp.int32, (8, r), 0)
    e32 = lax.broadcasted_iota(jnp.int32, (N_EXPERTS, r), 0).astype(F32)

    @pl.when(jnp.logical_and(phase == 0, i == 0))
    def _():
        cnt_ref[...] = jnp.zeros_like(cnt_ref)

    def rank_tile(s):
        lt = lg_ref[s * r:(s + 1) * r, :].T
        l0, l1, l2, l3 = lt[0:1], lt[1:2], lt[2:3], lt[3:4]
        gmax = jnp.maximum(jnp.maximum(l0, l1), jnp.maximum(l2, l3))
        gsum = (jnp.exp(l0 - gmax) + jnp.exp(l1 - gmax)) + (jnp.exp(l2 - gmax) + jnp.exp(l3 - gmax))
        p_g = 1.0 / gsum
        gsel = jnp.where(l0 == gmax, 0, jnp.where(l1 == gmax, 1, jnp.where(l2 == gmax, 2, 3)))
        eg = [lt[EXP_ROW0 + EXP_PER_GROUP * g:EXP_ROW0 + EXP_PER_GROUP * (g + 1)]
              for g in range(N_GROUPS)]
        sel = jnp.where(gsel == 0, eg[0], jnp.where(gsel == 1, eg[1],
                                                     jnp.where(gsel == 2, eg[2], eg[3])))
        m1 = jnp.max(sel, axis=0, keepdims=True)
        i1 = jnp.min(jnp.where(sel == m1, sub8, EXP_PER_GROUP), axis=0, keepdims=True)
        sel2 = jnp.where(sub8 == i1, -jnp.inf, sel)
        m2 = jnp.max(sel2, axis=0, keepdims=True)
        i2 = jnp.min(jnp.where(sel2 == m2, sub8, EXP_PER_GROUP), axis=0, keepdims=True)
        ex = jnp.exp(m2 - m1)
        g1 = 1.0 / (1.0 + ex)
        g2 = ex * g1
        eid1 = (gsel * EXP_PER_GROUP + i1).astype(F32)
        eid2 = (gsel * EXP_PER_GROUP + i2).astype(F32)
        oh1 = jnp.where(e32 == eid1, 1.0, 0.0)
        oh2 = jnp.where(e32 == eid2, 1.0, 0.0)
        cum1 = _dot(oh1.astype(BF16), utri_ref[...])
        cum2 = _dot(oh2.astype(BF16), utri_ref[...])
        tot1 = jnp.sum(oh1, axis=1, keepdims=True)
        tot2 = jnp.sum(oh2, axis=1, keepdims=True)
        base = cnt_ref[:, 0:1]
        rank1 = jnp.sum(oh1 * (base + cum1), axis=0, keepdims=True)
        rank2 = jnp.sum(oh2 * (base + tot1 + cum2), axis=0, keepdims=True)
        cnt_ref[...] = cnt_ref[...] + (tot1 + tot2)
        rec = jnp.where(sub8 == 0, eid1, jnp.where(sub8 == 1, eid2, jnp.where(
            sub8 == 2, rank1, jnp.where(sub8 == 3, rank2, jnp.where(
                sub8 == 4, g1 * p_g, jnp.where(sub8 == 5, g2 * p_g, 0.0))))))
        rec_ref[i * n_sub + s] = rec

    @pl.when(phase == 0)
    def _():
        for s in range(n_sub):
            rank_tile(s)

    @pl.when(phase == 1)
    def _():
        cnt = cnt_ref[...]
        padded = jnp.floor((cnt + (FFN_BLOCK - 1)) * (1.0 / FFN_BLOCK)) * FFN_BLOCK
        rows = lax.broadcasted_iota(jnp.int32, cnt.shape, 0)
        pad_end = padded
        for s in (1, 2, 4, 8, 16):
            pad_end = pad_end + jnp.where(rows >= s, pltpu.roll(pad_end, s, axis=0), 0.0)
        pad_start = (pad_end - padded)[:, 0:1]
        subw = lax.broadcasted_iota(jnp.int32, (ROUTE_COLS, r), 0)
        for s in range(n_sub):
            rec = rec_ref[i * n_sub + s]
            d1 = rec[2:3] + jnp.sum(jnp.where(e32 == rec[0:1], pad_start, 0.0), axis=0,
                                    keepdims=True)
            d2 = rec[3:4] + jnp.sum(jnp.where(e32 == rec[1:2], pad_start, 0.0), axis=0,
                                    keepdims=True)
            dest_ref[s] = jnp.where(sub8 == 0, d1, jnp.where(sub8 == 1, d2, 0.0)).astype(jnp.int32)
            gates = jnp.where(subw == 0, rec[4:5], jnp.where(subw == 1, rec[5:6], 0.0))
            gt_ref[s * r:(s + 1) * r, :] = gates.T
        nbp = binfo_ref.shape[1]
        blk_start = (lax.broadcasted_iota(jnp.int32, (N_EXPERTS, nbp), 1) * FFN_BLOCK).astype(F32)
        n_le = jnp.sum(jnp.where(pad_end[:, 0:1] <= blk_start, 1.0, 0.0), axis=0, keepdims=True)
        binfo_ref[...] = jnp.broadcast_to(n_le, binfo_ref.shape).astype(jnp.int32)
        lane = lax.broadcasted_iota(jnp.int32, cnt.shape, 1)
        einfo_ref[...] = jnp.where(lane == 0, pad_end - padded + cnt, pad_end).astype(jnp.int32)


def _route(logits, nbp):
    t = logits.shape[0]
    r = min(ROUTE_TILE, t)
    rr = min(ROUTE_SUBTILES * r, t)
    nt = t // rr
    utri = jnp.triu(jnp.ones((r, r), F32), k=1).astype(BF16)
    return pl.pallas_call(
        _route_kernel,
        grid=(2, nt),
        in_specs=[pl.BlockSpec((rr, ROUTE_COLS), lambda p, i: (i * (1 - p) + (nt - 1) * p, 0)),
                  pl.BlockSpec((r, r), lambda p, i: (0, 0))],
        out_specs=[pl.BlockSpec((rr // r, 8, r), lambda p, i: (i * p, 0, 0)),
                   pl.BlockSpec((rr, ROUTE_COLS), lambda p, i: (i * p, 0)),
                   pl.BlockSpec((8, nbp), lambda p, i: (0, 0)),
                   pl.BlockSpec((N_EXPERTS, LANES), lambda p, i: (0, 0))],
        out_shape=[jax.ShapeDtypeStruct((t // r, 8, r), jnp.int32),
                   jax.ShapeDtypeStruct((t, ROUTE_COLS), F32),
                   jax.ShapeDtypeStruct((8, nbp), jnp.int32),
                   jax.ShapeDtypeStruct((N_EXPERTS, LANES), jnp.int32)],
        scratch_shapes=[pltpu.VMEM((t // r, 8, r), F32),
                        pltpu.VMEM((N_EXPERTS, LANES), F32)],
        compiler_params=pltpu.CompilerParams(
            dimension_semantics=("arbitrary", "arbitrary"), vmem_limit_bytes=VMEM_LIMIT),
        name="route",
    )(logits, utri)


DMA_UNROLL = 8


def _token_copy(src, src_tok, dst, dst_tok, sem):
    return pltpu.make_async_copy(
        src.at[pl.ds(pl.multiple_of(src_tok * ROW_TILE, ROW_TILE), ROW_TILE)],
        dst.at[pl.ds(pl.multiple_of(dst_tok * ROW_TILE, ROW_TILE), ROW_TILE)], sem)


def _pad_copies(plo_ref, phi_ref, zero_ref, buf_hbm, sem):
    copies = []
    for e in range(N_EXPERTS):
        lo = plo_ref[e]
        n = phi_ref[e] - lo
        bit = FFN_BLOCK // 2
        while bit >= 1:
            first = lo + (n & ~(2 * bit - 1))
            copies.append(((n & bit) != 0, pltpu.make_async_copy(
                zero_ref.at[pl.ds(0, bit * ROW_TILE)],
                buf_hbm.at[pl.ds(pl.multiple_of(first * ROW_TILE, ROW_TILE), bit * ROW_TILE)],
                sem)))
            bit //= 2
    blk_rows = FFN_BLOCK * ROW_TILE
    n_blocks = buf_hbm.shape[0] // blk_rows
    n_used = phi_ref[N_EXPERTS - 1] // FFN_BLOCK
    for j in range(N_EXPERTS):
        for half in range(2):
            row0 = pl.multiple_of((n_used + j) * blk_rows + half * (blk_rows // 2), ROW_TILE)
            copies.append((n_used + j < n_blocks, pltpu.make_async_copy(
                zero_ref, buf_hbm.at[pl.ds(row0, blk_rows // 2)], sem)))
    return copies


def _scatter_kernel(plo_ref, phi_ref, dest_ref, h2_ref, buf_hbm, zero_ref, sem, zsem):
    r = dest_ref.shape[0] // 2

    @pl.when(pl.program_id(0) == 0)
    def _():
        zero_ref[...] = jnp.zeros_like(zero_ref)
        for cond, cp in _pad_copies(plo_ref, phi_ref, zero_ref, buf_hbm, zsem):
            pl.when(cond)(cp.start)
        for cond, cp in _pad_copies(plo_ref, phi_ref, zero_ref, buf_hbm, zsem):
            pl.when(cond)(cp.wait)

    def issue(g, carry):
        t0 = g * DMA_UNROLL
        idx = [[dest_ref[k * r + t0 + u] for k in range(2)] for u in range(DMA_UNROLL)]
        for u in range(DMA_UNROLL):
            for k in range(2):
                _token_copy(h2_ref, t0 + u, buf_hbm, idx[u][k], sem).start(priority=k)
        return carry

    lax.fori_loop(0, r // DMA_UNROLL, issue, 0)

    def drain(t, carry):
        _token_copy(h2_ref, 0, buf_hbm, 0, sem).wait()
        return carry

    lax.fori_loop(0, 2 * r, drain, 0, unroll=DMA_UNROLL)


def _scatter(pad_lo, pad_hi, dest, r, h2, n_rows):
    nt = dest.shape[0] // (2 * r)
    return pl.pallas_call(
        _scatter_kernel,
        grid_spec=pltpu.PrefetchScalarGridSpec(
            num_scalar_prefetch=2,
            grid=(nt,),
            in_specs=[pl.BlockSpec((2 * r,), lambda i, lo, hi: (i,), memory_space=pltpu.SMEM),
                      pl.BlockSpec((r * ROW_TILE, LANES), lambda i, lo, hi: (i, 0))],
            out_specs=pl.BlockSpec(memory_space=pl.ANY),
            scratch_shapes=[pltpu.VMEM((FFN_BLOCK // 2 * ROW_TILE, LANES), F32),
                            pltpu.SemaphoreType.DMA(()),
                            pltpu.SemaphoreType.DMA(())]),
        out_shape=jax.ShapeDtypeStruct((n_rows * ROW_TILE, LANES), F32),
        compiler_params=pltpu.CompilerParams(
            dimension_semantics=("arbitrary",), has_side_effects=True),
        name="scatter",
    )(pad_lo, pad_hi, dest, h2)


def _ffn_kernel(be_ref, nu_ref, x_ref, w1_ref, w3_ref, w2_ref, y_ref, w13_s, w2_s):
    del nu_ref
    i = pl.program_id(0)
    e = be_ref[i]
    used = e < N_EXPERTS
    fresh = jnp.logical_or(i == 0, e != be_ref[jnp.maximum(i - 1, 0)])

    half = D_EXPERT // 2

    @pl.when(jnp.logical_and(used, fresh))
    def _():
        for c in range(2):
            w13_s[c, :, :half] = w1_ref[0, :, c * half:(c + 1) * half].astype(BF16)
            w13_s[c, :, half:] = w3_ref[0, :, c * half:(c + 1) * half].astype(BF16)
        w2_s[...] = w2_ref[0].astype(BF16)

    @pl.when(used)
    def _():
        kw = 2 * LANES
        h = [None, None]
        for kc in range(w13_s.shape[1] // kw):
            xk = jnp.concatenate(
                [x_ref[pl.ds(2 * kc + j, FFN_BLOCK, stride=ROW_TILE), :] for j in range(2)],
                axis=1).astype(BF16)
            for c in range(2):
                part = _dot(xk, w13_s[c, kc * kw:(kc + 1) * kw, :])
                h[c] = part if h[c] is None else h[c] + part
        acts = []
        for c in range(2):
            a = h[c][:, :half]
            acts.append((a * jax.nn.sigmoid(a) * h[c][:, half:]).astype(BF16))
        for nc in range(y_ref.shape[0] // FFN_BLOCK // 2):
            cols = slice(nc * kw, (nc + 1) * kw)
            y = (_dot(acts[0], w2_s[0:half, cols]) + _dot(acts[1], w2_s[half:2 * half, cols]))
            for j in range(2):
                y_ref[pl.ds(2 * nc + j, FFN_BLOCK, stride=ROW_TILE), :] = (
                    y[:, j * LANES:(j + 1) * LANES])

    @pl.when(jnp.logical_not(used))
    def _():
        y_ref[...] = jnp.zeros_like(y_ref)


def _ffn(blk_e, n_used, buf, w1, w3, w2):
    nb = buf.shape[0] // (FFN_BLOCK * ROW_TILE)
    d = w1.shape[1]
    emap = lambda i, be, nu: (jnp.minimum(be[i], N_EXPERTS - 1), 0, 0)
    rows = pl.BlockSpec((FFN_BLOCK * ROW_TILE, LANES), lambda i, be, nu: (i, 0))
    rows_in = pl.BlockSpec((FFN_BLOCK * ROW_TILE, LANES),
                           lambda i, be, nu: (jnp.minimum(i, nu[0] - 1), 0))
    return pl.pallas_call(
        _ffn_kernel,
        grid_spec=pltpu.PrefetchScalarGridSpec(
            num_scalar_prefetch=2,
            grid=(nb,),
            in_specs=[rows_in,
                      pl.BlockSpec((1, d, D_EXPERT), emap),
                      pl.BlockSpec((1, d, D_EXPERT), emap),
                      pl.BlockSpec((1, D_EXPERT, d), emap)],
            out_specs=rows,
            scratch_shapes=[pltpu.VMEM((2, d, D_EXPERT), BF16),
                            pltpu.VMEM((D_EXPERT, d), BF16)]),
        out_shape=jax.ShapeDtypeStruct(buf.shape, F32),
        compiler_params=pltpu.CompilerParams(
            dimension_semantics=("arbitrary",), vmem_limit_bytes=VMEM_LIMIT),
        name="ffn",
    )(blk_e, n_used, buf, w1, w3, w2)


GATHER_SLOTS = 3


def _final_kernel(dest_ref, dn1_ref, dn2_ref, x1_ref, p_ref, gt_ref, ybuf_hbm, gple_ref, wpg_ref,
                  wple_ref, gfin_ref, out_ref, y_ref, x2_ref, h3_ref, g_ref, pb_ref, sems):
    i = pl.program_id(0)
    r = x1_ref.shape[0]
    slot = i % GATHER_SLOTS
    ahead = (i + GATHER_SLOTS - 1) % GATHER_SLOTS

    def issue_group(d_ref, s, t0):
        idx = [[d_ref[k * r + t0 + u] for k in range(2)] for u in range(DMA_UNROLL)]
        for u in range(DMA_UNROLL):
            for k in range(2):
                _token_copy(ybuf_hbm, idx[u][k], y_ref.at[s, k], t0 + u,
                            sems.at[s]).start(priority=k)

    def drain(s):
        def body(t, carry):
            _token_copy(ybuf_hbm, 0, y_ref.at[s, 0], 0, sems.at[s]).wait()
            return carry

        lax.fori_loop(0, 2 * r, body, 0, unroll=DMA_UNROLL)

    @pl.when(i == 0)
    def _():
        def body(g, carry):
            issue_group(dest_ref, 0, g * DMA_UNROLL)
            issue_group(dn1_ref, 1, g * DMA_UNROLL)
            return carry

        lax.fori_loop(0, r // DMA_UNROLL, body, 0)

    drain(slot)

    n_batch = 4
    per = r // n_batch

    def issue_batch(b):
        for t0 in range(b * per, (b + 1) * per, DMA_UNROLL):
            issue_group(dn2_ref, ahead, t0)

    n_chunk = 8
    rc = r // n_chunk
    issue_batch(0)
    for c in range(n_chunk):
        rows = pl.ds(c * rc, rc)
        x2 = (x1_ref[rows, :]
              + gt_ref[rows, 0:1] * _load_token_rows(y_ref.at[slot, 0], rc, c * rc)
              + gt_ref[rows, 1:2] * _load_token_rows(y_ref.at[slot, 1], rc, c * rc))
        x2_ref[rows, :] = x2
        h3_ref[rows, :] = _rms(x2, gple_ref[...]).astype(BF16)
    issue_batch(1)
    pb_ref[...] = p_ref[...].astype(BF16)
    for n in range(0, x2_ref.shape[1], 256):
        cols = slice(n, n + 256)
        g_ref[:, cols] = (jax.nn.sigmoid(_dot(h3_ref[...], wpg_ref[:, cols]))
                          * _dot(pb_ref[...], wple_ref[:, cols]))
        if n == 256:
            issue_batch(2)
    issue_batch(3)
    for c in range(n_chunk):
        rows = pl.ds(c * rc, rc)
        out_ref[rows, :] = _rms(x2_ref[rows, :] + g_ref[rows, :], gfin_ref[...])

    @pl.when(i + 1 == pl.num_programs(0))
    def _():
        for other in range(1, GATHER_SLOTS):
            drain((i + other) % GATHER_SLOTS)


def _final(dest, r, x1, p, gt, ybuf, g_ple, w_pg, w_ple, g_final):
    t, d = x1.shape
    nt = t // r

    def full(a):
        nd = a.ndim
        return pl.BlockSpec(a.shape, lambda i, _nd=nd: (0,) * _nd, pipeline_mode=pl.Buffered(1))

    return pl.pallas_call(
        _final_kernel,
        grid=(nt,),
        in_specs=[pl.BlockSpec((2 * r,), lambda i: (i,), memory_space=pltpu.SMEM),
                  pl.BlockSpec((2 * r,), lambda i: (jnp.minimum(i + 1, nt - 1),),
                               memory_space=pltpu.SMEM),
                  pl.BlockSpec((2 * r,), lambda i: (jnp.minimum(i + 2, nt - 1),),
                               memory_space=pltpu.SMEM),
                  pl.BlockSpec((r, d), lambda i: (i, 0)),
                  pl.BlockSpec((r, p.shape[1]), lambda i: (i, 0)),
                  pl.BlockSpec((r, ROUTE_COLS), lambda i: (i, 0)),
                  pl.BlockSpec(memory_space=pl.ANY),
                  full(g_ple), full(w_pg), full(w_ple), full(g_final)],
        out_specs=pl.BlockSpec((r, d), lambda i: (i, 0)),
        out_shape=jax.ShapeDtypeStruct((t, d), F32),
        scratch_shapes=[pltpu.VMEM((GATHER_SLOTS, 2, r * ROW_TILE, LANES), F32),
                        pltpu.VMEM((r, d), F32),
                        pltpu.VMEM((r, d), BF16),
                        pltpu.VMEM((r, d), F32),
                        pltpu.VMEM((r, p.shape[1]), BF16),
                        pltpu.SemaphoreType.DMA((GATHER_SLOTS,))],
        compiler_params=pltpu.CompilerParams(
            dimension_semantics=("arbitrary",), vmem_limit_bytes=VMEM_LIMIT),
        name="final",
    )(dest, dest, dest, x1, p, gt, ybuf, g_ple, w_pg, w_ple, g_final)


def kernel(x, p, g_mix, w_in, gm_ln_g, gm_ln_b, gm_w_sp, gm_b_sp, w_up_a, hg_lb_param, hg_norm_g,
           w_up_b, w_out, g_ffn, w_grp, w_exp, w1, w3, w2, g_ple, w_pg, w_ple, g_final):
    bsz, seq, d = x.shape
    t = bsz * seq
    row = lambda a: a.reshape(1, -1)

    assert w_in.shape[0] == 1 and hg_lb_param.shape[0] == 2, "single-layer block"
    i = 0
    w_rt = jnp.zeros((d, ROUTE_COLS), F32)
    w_rt = w_rt.at[:, 0:N_GROUPS].set(w_grp[i])
    w_rt = w_rt.at[:, EXP_ROW0:EXP_ROW0 + N_EXPERTS].set(w_exp[i])
    w_rt_hi = w_rt.astype(BF16)
    w_rt_lo = (w_rt - w_rt_hi.astype(F32)).astype(BF16)
    w_rt2 = jnp.concatenate([w_rt_hi, w_rt_lo], axis=1)
    bfull = jnp.repeat(gm_b_sp[i].T, GM_WIDTH // GM_GROUPS, axis=1)

    x1, h2, logits = _mixer(
        x, row(g_mix[i]), w_in[i].astype(BF16), row(gm_ln_g[i]), row(gm_ln_b[i]), gm_w_sp[i],
        bfull, w_up_a[i].astype(BF16), hg_lb_param, row(hg_norm_g[i]),
        w_up_b[i].astype(BF16), w_out[i].astype(BF16), row(g_ffn[i]), w_rt2, w_rt_hi)

    n_rows = 2 * t + N_EXPERTS * FFN_BLOCK
    nb = n_rows // FFN_BLOCK
    nbp = -(-nb // LANES) * LANES
    dest, gt, binfo, einfo = _route(logits.reshape(t, ROUTE_COLS), nbp)
    r = dest.shape[2]
    dest = dest[:, 0:2, :].reshape(-1)
    buf = _scatter(einfo[:, 0], einfo[:, 1], dest, r, h2, n_rows)
    n_used = einfo[N_EXPERTS - 1:, 1] // FFN_BLOCK
    ybuf = _ffn(binfo[0, :nb], n_used, buf, w1[i], w3[i], w2[i])
    out = _final(dest, r, x1.reshape(t, d), p[i].reshape(t, -1), gt, ybuf, row(g_ple[i]),
                 w_pg[i].astype(BF16), w_ple[i].astype(BF16), row(g_final))
    return out.reshape(bsz, seq, d)
```

```python
import functools

import jax
import jax.numpy as jnp
from jax import lax
from jax.experimental import pallas as pl
from jax.experimental.pallas import tpu as pltpu

F32 = jnp.float32
BF16 = jnp.bfloat16

EPS = 1e-6
GM_WIDTH = 512
GM_GROUPS = 8
GM_CHUNK = 128
HG_HEADS = 4
HG_KEY = 128
HG_CHUNK = 64
HG_KW = HG_HEADS * HG_KEY
N_GROUPS = 4
EXP_PER_GROUP = 8
N_EXPERTS = N_GROUPS * EXP_PER_GROUP
D_EXPERT = 512
LANES = 128
ROUTE_COLS = LANES
EXP_ROW0 = 8

MIX_TILE = 512
ROUTE_TILE = 512
ROUTE_SUBTILES = 4
FFN_BLOCK = 512
VMEM_LIMIT = 56 * 1024 * 1024


def _rms(x, g):
    return x * lax.rsqrt(jnp.mean(x * x, axis=-1, keepdims=True) + EPS) * g


def _dot(a, b):
    return jnp.dot(a, b, preferred_element_type=F32)


def _dot_nt(a, b):
    return lax.dot_general(a, b, (((1,), (1,)), ((), ())), preferred_element_type=F32)


def _dot_tn(a, b):
    return lax.dot_general(a, b, (((0,), (0,)), ((), ())), preferred_element_type=F32)


ROW_TILE = 8


def _store_token_rows(ref, val):
    n = val.shape[0]
    for c in range(ROW_TILE):
        ref[pl.ds(c, n, stride=ROW_TILE), :] = val[:, c * LANES:(c + 1) * LANES]


def _load_token_rows(ref, n, first=0):
    return jnp.concatenate(
        [ref[pl.ds(first * ROW_TILE + c, n, stride=ROW_TILE), :] for c in range(ROW_TILE)], axis=1)


def _mixer_kernel(x_ref, gmix_ref, win_ref, lng_ref, lnb_ref, wsp_ref, bfull_ref, wupa_ref,
                  lbp_ref, ng_ref, wupb_ref, wout_ref, gffn_ref, wrt2_ref, wrt1_ref,
                  x1_ref, h2_ref, lg_ref,
                  z_ref, h_ref, ya_ref, yb_ref, m_ref, st_ref, qd_ref, oi_ref, ds_ref, sc_ref,
                  dec_ref, b_ref, ki_ref, ke_ref, att_ref):
    ts = x_ref.shape[1]
    in_cols = win_ref.shape[1]

    @pl.when(pl.program_id(1) == 0)
    def _():
        st_ref[...] = jnp.zeros_like(st_ref)

    h_ref[...] = _rms(x_ref[0], gmix_ref[...]).astype(BF16)
    def in_proj(col0):
        for n in range(col0, col0 + 1024, 512):
            z_ref[:, n:n + 512] = _dot(h_ref[...], win_ref[:, n:n + 512])

    row = lax.broadcasted_iota(jnp.int32, (GM_CHUNK, GM_CHUNK), 0)
    col = lax.broadcasted_iota(jnp.int32, (GM_CHUNK, GM_CHUNK), 1)
    tril = row >= col
    low_half = col < (LANES // 2)
    ucols = slice(0, GM_WIDTH)
    ncols = slice(GM_WIDTH, 2 * GM_WIDTH)

    def gm_norm(c):
        rows = pl.ds(c * GM_CHUNK, GM_CHUNK)
        z_ref[rows, ucols] = jax.nn.gelu(z_ref[rows, ucols])
        v = jax.nn.gelu(z_ref[rows, ncols])
        d = v - jnp.mean(v, axis=-1, keepdims=True)
        var = jnp.mean(d * d, axis=-1, keepdims=True)
        z_ref[rows, ncols] = d * lax.rsqrt(var + EPS) * lng_ref[...] + lnb_ref[...]

    def gm_gate(c):
        rows = pl.ds(c * GM_CHUNK, GM_CHUNK)
        parts = []
        for p in range(GM_GROUPS // 2):
            vp = z_ref[rows, GM_WIDTH + p * LANES:GM_WIDTH + (p + 1) * LANES]
            v_lo = jnp.where(low_half, vp, 0.0).astype(BF16)
            v_hi = jnp.where(low_half, 0.0, vp).astype(BF16)
            w_lo = jnp.where(tril, wsp_ref[2 * p], 0.0).astype(BF16)
            w_hi = jnp.where(tril, wsp_ref[2 * p + 1], 0.0).astype(BF16)
            parts.append(_dot(w_lo, v_lo) + _dot(w_hi, v_hi))
        s = jnp.concatenate(parts, axis=1) + bfull_ref[...]
        ya_ref[rows, :] = (z_ref[rows, ucols] * s).astype(BF16)

    lbp = lbp_ref[...]
    lmax = jnp.maximum(lbp[0:1], lbp[1:2])
    e0 = jnp.exp(lbp[0:1] - lmax)
    e1 = jnp.exp(lbp[1:2] - lmax)
    lb = e0 / (e0 + e1)

    blk = 2 * HG_CHUNK
    brow = lax.broadcasted_iota(jnp.int32, (blk, blk), 0)
    bcol = lax.broadcasted_iota(jnp.int32, (blk, blk), 1)
    causal = jnp.logical_and(brow >= bcol, (brow < HG_CHUNK) == (bcol < HG_CHUNK))
    tri = jnp.where(causal, 1.0, 0.0).astype(BF16)
    first_chunk = lax.broadcasted_iota(jnp.int32, (blk, HG_KW), 0) < HG_CHUNK
    q0 = 2 * GM_WIDTH
    n_chunks = ts // HG_CHUNK

    n_rb = ts // blk
    qcols = slice(q0, q0 + HG_KW)
    fcols = slice(q0 + HG_KW, q0 + 2 * HG_KW)
    vcols = slice(q0 + 2 * HG_KW, q0 + 3 * HG_KW)
    heads = [slice(hh * HG_KEY, (hh + 1) * HG_KEY) for hh in range(HG_HEADS)]

    def hg_cumsum(rb):
        rows = pl.ds(rb * blk, blk)
        zq = z_ref[rows, qcols]
        f = lb + (1.0 - lb) * jax.nn.sigmoid(z_ref[rows, fcols])
        logf = jnp.log(f)
        lhi = logf.astype(BF16)
        llo = (logf - lhi.astype(F32)).astype(BF16)
        b_ref[rows, :] = _dot(tri, lhi) + _dot(tri, llo)
        z_ref[rows, qcols] = zq * jax.nn.sigmoid(zq)
        z_ref[rows, fcols] = 1.0 - f

    def hg_decay(rb):
        rows = pl.ds(rb * blk, blk)
        b = b_ref[rows, :]
        qf = z_ref[rows, qcols]
        k = z_ref[rows, fcols]
        b_mid = b[HG_CHUNK - 1:HG_CHUNK, :]
        b_end = b[blk - 1:blk, :]
        b_last = jnp.where(first_chunk, b_mid, b_end)
        qd_ref[rows, :] = (qf * jnp.exp(b)).astype(BF16)
        ki_ref[rows, :] = (k * jnp.exp(-b)).astype(BF16)
        ke_ref[rows, :] = (k * jnp.exp(b_last - b)).astype(BF16)
        dec_ref[2 * rb:2 * rb + 1, :] = jnp.exp(b_mid)
        dec_ref[2 * rb + 1:2 * rb + 2, :] = jnp.exp(b_end)

    def hg_scores(rb):
        rows = pl.ds(rb * blk, blk)
        for hh, sl in enumerate(heads):
            att = _dot_nt(qd_ref[rows, sl], ki_ref[rows, sl])
            att_ref[rb, hh] = jnp.where(causal, att, 0.0).astype(BF16)

    def hg_values(rb):
        rows = pl.ds(rb * blk, blk)
        vb = z_ref[rows, vcols].astype(BF16)
        for hh, sl in enumerate(heads):
            oi_ref[rows, sl] = _dot(att_ref[rb, hh], vb[:, sl])
        for cc in range(2):
            crows = pl.ds(rb * blk + cc * HG_CHUNK, HG_CHUNK)
            for hh, sl in enumerate(heads):
                ds_ref[2 * rb + cc, hh] = _dot_tn(
                    vb[cc * HG_CHUNK:(cc + 1) * HG_CHUNK, sl], ke_ref[crows, sl])

    def hg_scan(hh):
        sl = slice(hh * HG_KEY, (hh + 1) * HG_KEY)
        st = st_ref[hh]
        for c in range(n_chunks):
            sc_ref[c, hh] = st.astype(BF16)
            st = st * dec_ref[c:c + 1, sl] + ds_ref[c, hh]
        st_ref[hh] = st

    def hg_out(c):
        rows = pl.ds(c * HG_CHUNK, HG_CHUNK)
        og = jax.nn.sigmoid(z_ref[rows, q0 + 3 * HG_KW:q0 + 4 * HG_KW])
        outs = []
        for hh in range(HG_HEADS):
            sl = slice(hh * HG_KEY, (hh + 1) * HG_KEY)
            o = oi_ref[rows, sl] + _dot_nt(qd_ref[rows, sl], sc_ref[c, hh])
            o = o * lax.rsqrt(jnp.mean(o * o, axis=-1, keepdims=True) + EPS) * ng_ref[:, sl]
            outs.append(o * og[:, sl])
        yb_ref[rows, :] = jnp.concatenate(outs, axis=1).astype(BF16)

    n_gm = ts // GM_CHUNK
    in_proj(q0)
    in_proj(q0 + 2 * HG_KW)
    for rb in range(n_rb):
        hg_cumsum(rb)
    in_proj(0)
    for rb in range(n_rb):
        hg_decay(rb)
    for rb in range(n_rb):
        hg_scores(rb)
    in_proj(q0 + 4 * HG_KW)
    for c in range(n_gm):
        gm_norm(c)
    for rb in range(n_rb):
        hg_values(rb)
    in_proj(q0 + 4 * HG_KW + 1024)
    for c in range(n_gm):
        gm_gate(c)
    for hh in range(HG_HEADS):
        hg_scan(hh)
    for c in range(n_chunks):
        hg_out(c)

    ga0 = q0 + 4 * HG_KW
    gb0 = ga0 + x_ref.shape[2]
    for n in range(0, x_ref.shape[2], 512):
        ua = _dot(ya_ref[...], wupa_ref[:, n:n + 512])
        ub = _dot(yb_ref[...], wupb_ref[:, n:n + 512])
        merged = (jax.nn.sigmoid(z_ref[:, ga0 + n:ga0 + n + 512]) * ua
                  + jax.nn.sigmoid(z_ref[:, gb0 + n:gb0 + n + 512]) * ub)
        m_ref[:, n:n + 512] = merged.astype(BF16)
    x1 = x_ref[0] + _dot(m_ref[...], wout_ref[...])
    x1_ref[0] = x1
    h2 = _rms(x1, gffn_ref[...])
    _store_token_rows(h2_ref, h2)
    hi = h2.astype(BF16)
    lo = (h2 - hi.astype(F32)).astype(BF16)
    l2 = _dot(hi, wrt2_ref[...])
    lg_ref[0] = l2[:, :ROUTE_COLS] + l2[:, ROUTE_COLS:] + _dot(lo, wrt1_ref[...])


def _mixer(x, g_mix, w_in, ln_g, ln_b, w_sp, bfull, w_up_a, lbp, norm_g, w_up_b, w_out, g_ffn,
           w_rt2, w_rt1):
    bsz, seq, d = x.shape
    ts = min(MIX_TILE, seq)
    in_cols = w_in.shape[1]

    def full(a):
        nd = a.ndim
        return pl.BlockSpec(a.shape, lambda b, j, _nd=nd: (0,) * _nd, pipeline_mode=pl.Buffered(1))

    consts = (g_mix, w_in, ln_g, ln_b, w_sp, bfull, w_up_a, lbp, norm_g, w_up_b, w_out, g_ffn,
              w_rt2, w_rt1)
    assert d == ROW_TILE * LANES
    nj = seq // ts
    tile = lambda w: pl.BlockSpec((1, ts, w), lambda b, j: (b, j, 0))
    return pl.pallas_call(
        _mixer_kernel,
        grid=(bsz, nj),
        in_specs=[tile(d)] + [full(a) for a in consts],
        out_specs=[tile(d),
                   pl.BlockSpec((ts * ROW_TILE, LANES), lambda b, j: (b * nj + j, 0)),
                   tile(ROUTE_COLS)],
        out_shape=[jax.ShapeDtypeStruct((bsz, seq, d), F32),
                   jax.ShapeDtypeStruct((bsz * seq * ROW_TILE, LANES), F32),
                   jax.ShapeDtypeStruct((bsz, seq, ROUTE_COLS), F32)],
        scratch_shapes=[pltpu.VMEM((ts, in_cols), F32),
                        pltpu.VMEM((ts, d), BF16),
                        pltpu.VMEM((ts, GM_WIDTH), BF16),
                        pltpu.VMEM((ts, HG_KW), BF16),
                        pltpu.VMEM((ts, d), BF16),
                        pltpu.VMEM((HG_HEADS, HG_KEY, HG_KEY), F32),
                        pltpu.VMEM((ts, HG_KW), BF16),
                        pltpu.VMEM((ts, HG_KW), F32),
                        pltpu.VMEM((ts // HG_CHUNK, HG_HEADS, HG_KEY, HG_KEY), F32),
                        pltpu.VMEM((ts // HG_CHUNK, HG_HEADS, HG_KEY, HG_KEY), BF16),
                        pltpu.VMEM((ts // HG_CHUNK, HG_KW), F32),
                        pltpu.VMEM((ts, HG_KW), F32),
                        pltpu.VMEM((ts, HG_KW), BF16),
                        pltpu.VMEM((ts, HG_KW), BF16),
                        pltpu.VMEM((ts // (2 * HG_CHUNK), HG_HEADS, 2 * HG_CHUNK, 2 * HG_CHUNK),
                                   BF16)],
        compiler_params=pltpu.CompilerParams(
            dimension_semantics=("arbitrary", "arbitrary"), vmem_limit_bytes=VMEM_LIMIT),
        name="mixer",
    )(x, *consts)


def _route_kernel(lg_ref, utri_ref, dest_ref, gt_ref, binfo_ref, einfo_ref, rec_ref, cnt_ref):
    phase = pl.program_id(0)
    i = pl.program_id(1)
    r = utri_ref.shape[0]
    n_sub = lg_ref.shape[0] // r
    sub8 = lax.broadcasted_iota(jnp.int32, (8, r), 0)
    e32 = lax.broadcasted_iota(jnp.int32, (N_EXPERTS, r), 0).astype(F32)

    @pl.when(jnp.logical_and(phase == 0, i == 0))
    def _():
        cnt_ref[...] = jnp.zeros_like(cnt_ref)

    def rank_tile(s):
        lt = lg_ref[s * r:(s + 1) * r, :].T
        l0, l1, l2, l3 = lt[0:1], lt[1:2], lt[2:3], lt[3:4]
        gmax = jnp.maximum(jnp.maximum(l0, l1), jnp.maximum(l2, l3))
        gsum = (jnp.exp(l0 - gmax) + jnp.exp(l1 - gmax)) + (jnp.exp(l2 - gmax) + jnp.exp(l3 - gmax))
        p_g = 1.0 / gsum
        gsel = jnp.where(l0 == gmax, 0, jnp.where(l1 == gmax, 1, jnp.where(l2 == gmax, 2, 3)))
        eg = [lt[EXP_ROW0 + EXP_PER_GROUP * g:EXP_ROW0 + EXP_PER_GROUP * (g + 1)]
              for g in range(N_GROUPS)]
        sel = jnp.where(gsel == 0, eg[0], jnp.where(gsel == 1, eg[1],
                                                     jnp.where(gsel == 2, eg[2], eg[3])))
        m1 = jnp.max(sel, axis=0, keepdims=True)
        i1 = jnp.min(jnp.where(sel == m1, sub8, EXP_PER_GROUP), axis=0, keepdims=True)
        sel2 = jnp.where(sub8 == i1, -jnp.inf, sel)
        m2 = jnp.max(sel2, axis=0, keepdims=True)
        i2 = jnp.min(jnp.where(sel2 == m2, sub8, EXP_PER_GROUP), axis=0, keepdims=True)
        ex = jnp.exp(m2 - m1)
        g1 = 1.0 / (1.0 + ex)
        g2 = ex * g1
        eid1 = (gsel * EXP_PER_GROUP + i1).astype(F32)
        eid2 = (gsel * EXP_PER_GROUP + i2).astype(F32)
        oh1 = jnp.where(e32 == eid1, 1.0, 0.0)
        oh2 = jnp.where(e32 == eid2, 1.0, 0.0)
        cum1 = _dot(oh1.astype(BF16), utri_ref[...])
        cum2 = _dot(oh2.astype(BF16), utri_ref[...])
        tot1 = jnp.sum(oh1, axis=1, keepdims=True)
        tot2 = jnp.sum(oh2, axis=1, keepdims=True)
        base = cnt_ref[:, 0:1]
        rank1 = jnp.sum(oh1 * (base + cum1), axis=0, keepdims=True)
        rank2 = jnp.sum(oh2 * (base + tot1 + cum2), axis=0, keepdims=True)
        cnt_ref[...] = cnt_ref[...] + (tot1 + tot2)
        rec = jnp.where(sub8 == 0, eid1, jnp.where(sub8 == 1, eid2, jnp.where(
            sub8 == 2, rank1, jnp.where(sub8 == 3, rank2, jnp.where(
                sub8 == 4, g1 * p_g, jnp.where(sub8 == 5, g2 * p_g, 0.0))))))
        rec_ref[i * n_sub + s] = rec

    @pl.when(phase == 0)
    def _():
        for s in range(n_sub):
            rank_tile(s)

    @pl.when(phase == 1)
    def _():
        cnt = cnt_ref[...]
        padded = jnp.floor((cnt + (FFN_BLOCK - 1)) * (1.0 / FFN_BLOCK)) * FFN_BLOCK
        rows = lax.broadcasted_iota(jnp.int32, cnt.shape, 0)
        pad_end = padded
        for s in (1, 2, 4, 8, 16):
            pad_end = pad_end + jnp.where(rows >= s, pltpu.roll(pad_end, s, axis=0), 0.0)
        pad_start = (pad_end - padded)[:, 0:1]
        subw = lax.broadcasted_iota(jnp.int32, (ROUTE_COLS, r), 0)
        for s in range(n_sub):
            rec = rec_ref[i * n_sub + s]
            d1 = rec[2:3] + jnp.sum(jnp.where(e32 == rec[0:1], pad_start, 0.0), axis=0,
                                    keepdims=True)
            d2 = rec[3:4] + jnp.sum(jnp.where(e32 == rec[1:2], pad_start, 0.0), axis=0,
                                    keepdims=True)
            dest_ref[s] = jnp.where(sub8 == 0, d1, jnp.where(sub8 == 1, d2, 0.0)).astype(jnp.int32)
            gates = jnp.where(subw == 0, rec[4:5], jnp.where(subw == 1, rec[5:6], 0.0))
            gt_ref[s * r:(s + 1) * r, :] = gates.T
        nbp = binfo_ref.shape[1]
        blk_start = (lax.broadcasted_iota(jnp.int32, (N_EXPERTS, nbp), 1) * FFN_BLOCK).astype(F32)
        n_le = jnp.sum(jnp.where(pad_end[:, 0:1] <= blk_start, 1.0, 0.0), axis=0, keepdims=True)
        binfo_ref[...] = jnp.broadcast_to(n_le, binfo_ref.shape).astype(jnp.int32)
        lane = lax.broadcasted_iota(jnp.int32, cnt.shape, 1)
        einfo_ref[...] = jnp.where(lane == 0, pad_end - padded + cnt, pad_end).astype(jnp.int32)


def _route(logits, nbp):
    t = logits.shape[0]
    r = min(ROUTE_TILE, t)
    rr = min(ROUTE_SUBTILES * r, t)
    nt = t // rr
    utri = jnp.triu(jnp.ones((r, r), F32), k=1).astype(BF16)
    return pl.pallas_call(
        _route_kernel,
        grid=(2, nt),
        in_specs=[pl.BlockSpec((rr, ROUTE_COLS), lambda p, i: (i * (1 - p) + (nt - 1) * p, 0)),
                  pl.BlockSpec((r, r), lambda p, i: (0, 0))],
        out_specs=[pl.BlockSpec((rr // r, 8, r), lambda p, i: (i * p, 0, 0)),
                   pl.BlockSpec((rr, ROUTE_COLS), lambda p, i: (i * p, 0)),
                   pl.BlockSpec((8, nbp), lambda p, i: (0, 0)),
                   pl.BlockSpec((N_EXPERTS, LANES), lambda p, i: (0, 0))],
        out_shape=[jax.ShapeDtypeStruct((t // r, 8, r), jnp.int32),
                   jax.ShapeDtypeStruct((t, ROUTE_COLS), F32),
                   jax.ShapeDtypeStruct((8, nbp), jnp.int32),
                   jax.ShapeDtypeStruct((N_EXPERTS, LANES), jnp.int32)],
        scratch_shapes=[pltpu.VMEM((t // r, 8, r), F32),
                        pltpu.VMEM((N_EXPERTS, LANES), F32)],
        compiler_params=pltpu.CompilerParams(
            dimension_semantics=("arbitrary", "arbitrary"), vmem_limit_bytes=VMEM_LIMIT),
        name="route",
    )(logits, utri)


DMA_UNROLL = 8


def _token_copy(src, src_tok, dst, dst_tok, sem):
    return pltpu.make_async_copy(
        src.at[pl.ds(pl.multiple_of(src_tok * ROW_TILE, ROW_TILE), ROW_TILE)],
        dst.at[pl.ds(pl.multiple_of(dst_tok * ROW_TILE, ROW_TILE), ROW_TILE)], sem)


def _pad_copies(plo_ref, phi_ref, zero_ref, buf_hbm, sem):
    copies = []
    for e in range(N_EXPERTS):
        lo = plo_ref[e]
        n = phi_ref[e] - lo
        bit = FFN_BLOCK // 2
        while bit >= 1:
            first = lo + (n & ~(2 * bit - 1))
            copies.append(((n & bit) != 0, pltpu.make_async_copy(
                zero_ref.at[pl.ds(0, bit * ROW_TILE)],
                buf_hbm.at[pl.ds(pl.multiple_of(first * ROW_TILE, ROW_TILE), bit * ROW_TILE)],
                sem)))
            bit //= 2
    blk_rows = FFN_BLOCK * ROW_TILE
    n_blocks = buf_hbm.shape[0] // blk_rows
    n_used = phi_ref[N_EXPERTS - 1] // FFN_BLOCK
    for j in range(N_EXPERTS):
        for half in range(2):
            row0 = pl.multiple_of((n_used + j) * blk_rows + half * (blk_rows // 2), ROW_TILE)
            copies.append((n_used + j < n_blocks, pltpu.make_async_copy(
                zero_ref, buf_hbm.at[pl.ds(row0, blk_rows // 2)], sem)))
    return copies


def _scatter_kernel(plo_ref, phi_ref, dest_ref, h2_ref, buf_hbm, zero_ref, sem, zsem):
    r = dest_ref.shape[0] // 2

    @pl.when(pl.program_id(0) == 0)
    def _():
        zero_ref[...] = jnp.zeros_like(zero_ref)
        for cond, cp in _pad_copies(plo_ref, phi_ref, zero_ref, buf_hbm, zsem):
            pl.when(cond)(cp.start)
        for cond, cp in _pad_copies(plo_ref, phi_ref, zero_ref, buf_hbm, zsem):
            pl.when(cond)(cp.wait)

    def issue(g, carry):
        t0 = g * DMA_UNROLL
        idx = [[dest_ref[k * r + t0 + u] for k in range(2)] for u in range(DMA_UNROLL)]
        for u in range(DMA_UNROLL):
            for k in range(2):
                _token_copy(h2_ref, t0 + u, buf_hbm, idx[u][k], sem).start(priority=k)
        return carry

    lax.fori_loop(0, r // DMA_UNROLL, issue, 0)

    def drain(t, carry):
        _token_copy(h2_ref, 0, buf_hbm, 0, sem).wait()
        return carry

    lax.fori_loop(0, 2 * r, drain, 0, unroll=DMA_UNROLL)


def _scatter(pad_lo, pad_hi, dest, r, h2, n_rows):
    nt = dest.shape[0] // (2 * r)
    return pl.pallas_call(
        _scatter_kernel,
        grid_spec=pltpu.PrefetchScalarGridSpec(
            num_scalar_prefetch=2,
            grid=(nt,),
            in_specs=[pl.BlockSpec((2 * r,), lambda i, lo, hi: (i,), memory_space=pltpu.SMEM),
                      pl.BlockSpec((r * ROW_TILE, LANES), lambda i, lo, hi: (i, 0))],
            out_specs=pl.BlockSpec(memory_space=pl.ANY),
            scratch_shapes=[pltpu.VMEM((FFN_BLOCK // 2 * ROW_TILE, LANES), F32),
                            pltpu.SemaphoreType.DMA(()),
                            pltpu.SemaphoreType.DMA(())]),
        out_shape=jax.ShapeDtypeStruct((n_rows * ROW_TILE, LANES), F32),
        compiler_params=pltpu.CompilerParams(
            dimension_semantics=("arbitrary",), has_side_effects=True),
        name="scatter",
    )(pad_lo, pad_hi, dest, h2)


FFN_SLOTS = 3


def _ffn_kernel(be_ref, nu_ref, x_hbm, w1_ref, w3_ref, w2_ref, y_ref, w13_s, w2_s, x_s, xsems):
    i = pl.program_id(0)
    blk_rows = FFN_BLOCK * ROW_TILE

    def block_copy(blk):
        src = jnp.minimum(blk, nu_ref[0] - 1)
        return pltpu.make_async_copy(
            x_hbm.at[pl.ds(pl.multiple_of(src * blk_rows, blk_rows), blk_rows)],
            x_s.at[blk % FFN_SLOTS], xsems.at[blk % FFN_SLOTS])

    @pl.when(i == 0)
    def _():
        block_copy(0).start()
        block_copy(1).start()

    block_copy(i + 2).start()
    block_copy(i).wait()
    x_ref = x_s.at[i % FFN_SLOTS]

    @pl.when(i + 1 == pl.num_programs(0))
    def _():
        block_copy(i + 1).wait()
        block_copy(i + 2).wait()

    e = be_ref[i]
    used = e < N_EXPERTS
    fresh = jnp.logical_or(i == 0, e != be_ref[jnp.maximum(i - 1, 0)])

    half = D_EXPERT // 2

    @pl.when(jnp.logical_and(used, fresh))
    def _():
        for c in range(2):
            w13_s[c, :, :half] = w1_ref[0, :, c * half:(c + 1) * half].astype(BF16)
            w13_s[c, :, half:] = w3_ref[0, :, c * half:(c + 1) * half].astype(BF16)
        w2_s[...] = w2_ref[0].astype(BF16)

    @pl.when(used)
    def _():
        kw = 2 * LANES
        h = [None, None]
        for kc in range(w13_s.shape[1] // kw):
            xk = jnp.concatenate(
                [x_ref[pl.ds(2 * kc + j, FFN_BLOCK, stride=ROW_TILE), :] for j in range(2)],
                axis=1).astype(BF16)
            for c in range(2):
                part = _dot(xk, w13_s[c, kc * kw:(kc + 1) * kw, :])
                h[c] = part if h[c] is None else h[c] + part
        acts = []
        for c in range(2):
            a = h[c][:, :half]
            acts.append((a * jax.nn.sigmoid(a) * h[c][:, half:]).astype(BF16))
        for nc in range(y_ref.shape[0] // FFN_BLOCK // 2):
            cols = slice(nc * kw, (nc + 1) * kw)
            y = (_dot(acts[0], w2_s[0:half, cols]) + _dot(acts[1], w2_s[half:2 * half, cols]))
            for j in range(2):
                y_ref[pl.ds(2 * nc + j, FFN_BLOCK, stride=ROW_TILE), :] = (
                    y[:, j * LANES:(j + 1) * LANES])

    @pl.when(jnp.logical_not(used))
    def _():
        y_ref[...] = jnp.zeros_like(y_ref)


def _ffn(blk_e, n_used, buf, w1, w3, w2):
    nb = buf.shape[0] // (FFN_BLOCK * ROW_TILE)
    d = w1.shape[1]
    emap = lambda i, be, nu: (jnp.minimum(be[i], N_EXPERTS - 1), 0, 0)
    rows = pl.BlockSpec((FFN_BLOCK * ROW_TILE, LANES), lambda i, be, nu: (i, 0))
    rows_in = pl.BlockSpec(memory_space=pl.ANY)
    return pl.pallas_call(
        _ffn_kernel,
        grid_spec=pltpu.PrefetchScalarGridSpec(
            num_scalar_prefetch=2,
            grid=(nb,),
            in_specs=[rows_in,
                      pl.BlockSpec((1, d, D_EXPERT), emap),
                      pl.BlockSpec((1, d, D_EXPERT), emap),
                      pl.BlockSpec((1, D_EXPERT, d), emap)],
            out_specs=rows,
            scratch_shapes=[pltpu.VMEM((2, d, D_EXPERT), BF16),
                            pltpu.VMEM((D_EXPERT, d), BF16),
                            pltpu.VMEM((FFN_SLOTS, FFN_BLOCK * ROW_TILE, LANES), F32),
                            pltpu.SemaphoreType.DMA((FFN_SLOTS,))]),
        out_shape=jax.ShapeDtypeStruct(buf.shape, F32),
        compiler_params=pltpu.CompilerParams(
            dimension_semantics=("arbitrary",), vmem_limit_bytes=VMEM_LIMIT),
        name="ffn",
    )(blk_e, n_used, buf, w1, w3, w2)


GATHER_SLOTS = 3


def _final_kernel(dest_ref, dn1_ref, dn2_ref, x1_ref, p_ref, gt_ref, ybuf_hbm, gple_ref, wpg_ref,
                  wple_ref, gfin_ref, out_ref, y_ref, x2_ref, h3_ref, g_ref, pb_ref, sems):
    i = pl.program_id(0)
    r = x1_ref.shape[0]
    slot = i % GATHER_SLOTS
    ahead = (i + GATHER_SLOTS - 1) % GATHER_SLOTS

    def issue_group(d_ref, s, t0):
        idx = [[d_ref[k * r + t0 + u] for k in range(2)] for u in range(DMA_UNROLL)]
        for u in range(DMA_UNROLL):
            for k in range(2):
                _token_copy(ybuf_hbm, idx[u][k], y_ref.at[s, k], t0 + u,
                            sems.at[s]).start(priority=k)

    def drain(s):
        def body(t, carry):
            _token_copy(ybuf_hbm, 0, y_ref.at[s, 0], 0, sems.at[s]).wait()
            return carry

        lax.fori_loop(0, 2 * r, body, 0, unroll=DMA_UNROLL)

    @pl.when(i == 0)
    def _():
        def body(g, carry):
            issue_group(dest_ref, 0, g * DMA_UNROLL)
            issue_group(dn1_ref, 1, g * DMA_UNROLL)
            return carry

        lax.fori_loop(0, r // DMA_UNROLL, body, 0)

    drain(slot)

    n_batch = 4
    per = r // n_batch

    def issue_batch(b):
        for t0 in range(b * per, (b + 1) * per, DMA_UNROLL):
            issue_group(dn2_ref, ahead, t0)

    n_chunk = 8
    rc = r // n_chunk
    issue_batch(0)
    for c in range(n_chunk):
        rows = pl.ds(c * rc, rc)
        x2 = (x1_ref[rows, :]
              + gt_ref[rows, 0:1] * _load_token_rows(y_ref.at[slot, 0], rc, c * rc)
              + gt_ref[rows, 1:2] * _load_token_rows(y_ref.at[slot, 1], rc, c * rc))
        x2_ref[rows, :] = x2
        h3_ref[rows, :] = _rms(x2, gple_ref[...]).astype(BF16)
    issue_batch(1)
    pb_ref[...] = p_ref[...].astype(BF16)
    for n in range(0, x2_ref.shape[1], 256):
        cols = slice(n, n + 256)
        g_ref[:, cols] = (jax.nn.sigmoid(_dot(h3_ref[...], wpg_ref[:, cols]))
                          * _dot(pb_ref[...], wple_ref[:, cols]))
        if n == 256:
            issue_batch(2)
    issue_batch(3)
    for c in range(n_chunk):
        rows = pl.ds(c * rc, rc)
        out_ref[rows, :] = _rms(x2_ref[rows, :] + g_ref[rows, :], gfin_ref[...])

    @pl.when(i + 1 == pl.num_programs(0))
    def _():
        for other in range(1, GATHER_SLOTS):
            drain((i + other) % GATHER_SLOTS)


def _final(dest, r, x1, p, gt, ybuf, g_ple, w_pg, w_ple, g_final):
    t, d = x1.shape
    nt = t // r

    def full(a):
        nd = a.ndim
        return pl.BlockSpec(a.shape, lambda i, _nd=nd: (0,) * _nd, pipeline_mode=pl.Buffered(1))

    return pl.pallas_call(
        _final_kernel,
        grid=(nt,),
        in_specs=[pl.BlockSpec((2 * r,), lambda i: (i,), memory_space=pltpu.SMEM),
                  pl.BlockSpec((2 * r,), lambda i: (jnp.minimum(i + 1, nt - 1),),
                               memory_space=pltpu.SMEM),
                  pl.BlockSpec((2 * r,), lambda i: (jnp.minimum(i + 2, nt - 1),),
                               memory_space=pltpu.SMEM),
                  pl.BlockSpec((r, d), lambda i: (i, 0)),
                  pl.BlockSpec((r, p.shape[1]), lambda i: (i, 0)),
                  pl.BlockSpec((r, ROUTE_COLS), lambda i: (i, 0)),
                  pl.BlockSpec(memory_space=pl.ANY),
                  full(g_ple), full(w_pg), full(w_ple), full(g_final)],
        out_specs=pl.BlockSpec((r, d), lambda i: (i, 0)),
        out_shape=jax.ShapeDtypeStruct((t, d), F32),
        scratch_shapes=[pltpu.VMEM((GATHER_SLOTS, 2, r * ROW_TILE, LANES), F32),
                        pltpu.VMEM((r, d), F32),
                        pltpu.VMEM((r, d), BF16),
                        pltpu.VMEM((r, d), F32),
                        pltpu.VMEM((r, p.shape[1]), BF16),
                        pltpu.SemaphoreType.DMA((GATHER_SLOTS,))],
        compiler_params=pltpu.CompilerParams(
            dimension_semantics=("arbitrary",), vmem_limit_bytes=VMEM_LIMIT),
        name="final",
    )(dest, dest, dest, x1, p, gt, ybuf, g_ple, w_pg, w_ple, g_final)


def kernel(x, p, g_mix, w_in, gm_ln_g, gm_ln_b, gm_w_sp, gm_b_sp, w_up_a, hg_lb_param, hg_norm_g,
           w_up_b, w_out, g_ffn, w_grp, w_exp, w1, w3, w2, g_ple, w_pg, w_ple, g_final):
    bsz, seq, d = x.shape
    t = bsz * seq
    row = lambda a: a.reshape(1, -1)

    assert w_in.shape[0] == 1 and hg_lb_param.shape[0] == 2, "single-layer block"
    i = 0
    w_rt = jnp.zeros((d, ROUTE_COLS), F32)
    w_rt = w_rt.at[:, 0:N_GROUPS].set(w_grp[i])
    w_rt = w_rt.at[:, EXP_ROW0:EXP_ROW0 + N_EXPERTS].set(w_exp[i])
    w_rt_hi = w_rt.astype(BF16)
    w_rt_lo = (w_rt - w_rt_hi.astype(F32)).astype(BF16)
    w_rt2 = jnp.concatenate([w_rt_hi, w_rt_lo], axis=1)
    bfull = jnp.repeat(gm_b_sp[i].T, GM_WIDTH // GM_GROUPS, axis=1)

    x1, h2, logits = _mixer(
        x, row(g_mix[i]), w_in[i].astype(BF16), row(gm_ln_g[i]), row(gm_ln_b[i]), gm_w_sp[i],
        bfull, w_up_a[i].astype(BF16), hg_lb_param, row(hg_norm_g[i]),
        w_up_b[i].astype(BF16), w_out[i].astype(BF16), row(g_ffn[i]), w_rt2, w_rt_hi)

    n_rows = 2 * t + N_EXPERTS * FFN_BLOCK
    nb = n_rows // FFN_BLOCK
    nbp = -(-nb // LANES) * LANES
    dest, gt, binfo, einfo = _route(logits.reshape(t, ROUTE_COLS), nbp)
    r = dest.shape[2]
    dest = dest[:, 0:2, :].reshape(-1)
    buf = _scatter(einfo[:, 0], einfo[:, 1], dest, r, h2, n_rows)
    n_used = einfo[N_EXPERTS - 1:, 1] // FFN_BLOCK
    ybuf = _ffn(binfo[0, :nb], n_used, buf, w1[i], w3[i], w2[i])
    out = _final(dest, r, x1.reshape(t, d), p[i].reshape(t, -1), gt, ybuf, row(g_ple[i]),
                 w_pg[i].astype(BF16), w_ple[i].astype(BF16), row(g_final))
    return out.reshape(bsz, seq, d)
```

```python
import functools

import jax
import jax.numpy as jnp
from jax import lax
from jax.experimental import pallas as pl
from jax.experimental.pallas import tpu as pltpu

F32 = jnp.float32
BF16 = jnp.bfloat16

EPS = 1e-6
GM_WIDTH = 512
GM_GROUPS = 8
GM_CHUNK = 128
HG_HEADS = 4
HG_KEY = 128
HG_CHUNK = 64
HG_KW = HG_HEADS * HG_KEY
N_GROUPS = 4
EXP_PER_GROUP = 8
N_EXPERTS = N_GROUPS * EXP_PER_GROUP
D_EXPERT = 512
LANES = 128
ROUTE_COLS = LANES
EXP_ROW0 = 8

MIX_TILE = 512
ROUTE_TILE = 512
ROUTE_SUBTILES = 4
FFN_BLOCK = 512
VMEM_LIMIT = 56 * 1024 * 1024


def _rms(x, g):
    return x * lax.rsqrt(jnp.mean(x * x, axis=-1, keepdims=True) + EPS) * g


def _dot(a, b):
    return jnp.dot(a, b, preferred_element_type=F32)


def _dot_nt(a, b):
    return lax.dot_general(a, b, (((1,), (1,)), ((), ())), preferred_element_type=F32)


def _dot_tn(a, b):
    return lax.dot_general(a, b, (((0,), (0,)), ((), ())), preferred_element_type=F32)


ROW_TILE = 8


def _store_token_rows(ref, val):
    n = val.shape[0]
    for c in range(ROW_TILE):
        ref[pl.ds(c, n, stride=ROW_TILE), :] = val[:, c * LANES:(c + 1) * LANES]


def _load_token_rows(ref, n, first=0):
    return jnp.concatenate(
        [ref[pl.ds(first * ROW_TILE + c, n, stride=ROW_TILE), :] for c in range(ROW_TILE)], axis=1)


def _mixer_kernel(x_ref, gmix_ref, win_ref, lng_ref, lnb_ref, wsp_ref, bfull_ref, wupa_ref,
                  lbp_ref, ng_ref, wupb_ref, wout_ref, gffn_ref, wrt2_ref, wrt1_ref,
                  x1_ref, h2_ref, lg_ref,
                  z_ref, h_ref, ya_ref, yb_ref, m_ref, st_ref, qd_ref, oi_ref, ds_ref, sc_ref,
                  dec_ref, b_ref, ki_ref, ke_ref, att_ref):
    ts = x_ref.shape[1]
    in_cols = win_ref.shape[1]

    @pl.when(pl.program_id(1) == 0)
    def _():
        st_ref[...] = jnp.zeros_like(st_ref)

    h_ref[...] = _rms(x_ref[0], gmix_ref[...]).astype(BF16)
    def in_proj(col0):
        for n in range(col0, col0 + 1024, 512):
            z_ref[:, n:n + 512] = _dot(h_ref[...], win_ref[:, n:n + 512])

    row = lax.broadcasted_iota(jnp.int32, (GM_CHUNK, GM_CHUNK), 0)
    col = lax.broadcasted_iota(jnp.int32, (GM_CHUNK, GM_CHUNK), 1)
    tril = row >= col
    low_half = col < (LANES // 2)
    ucols = slice(0, GM_WIDTH)
    ncols = slice(GM_WIDTH, 2 * GM_WIDTH)

    def gm_norm(c):
        rows = pl.ds(c * GM_CHUNK, GM_CHUNK)
        z_ref[rows, ucols] = jax.nn.gelu(z_ref[rows, ucols])
        v = jax.nn.gelu(z_ref[rows, ncols])
        d = v - jnp.mean(v, axis=-1, keepdims=True)
        var = jnp.mean(d * d, axis=-1, keepdims=True)
        z_ref[rows, ncols] = d * lax.rsqrt(var + EPS) * lng_ref[...] + lnb_ref[...]

    def gm_gate(c):
        rows = pl.ds(c * GM_CHUNK, GM_CHUNK)
        parts = []
        for p in range(GM_GROUPS // 2):
            vp = z_ref[rows, GM_WIDTH + p * LANES:GM_WIDTH + (p + 1) * LANES]
            v_lo = jnp.where(low_half, vp, 0.0).astype(BF16)
            v_hi = jnp.where(low_half, 0.0, vp).astype(BF16)
            w_lo = jnp.where(tril, wsp_ref[2 * p], 0.0).astype(BF16)
            w_hi = jnp.where(tril, wsp_ref[2 * p + 1], 0.0).astype(BF16)
            parts.append(_dot(w_lo, v_lo) + _dot(w_hi, v_hi))
        s = jnp.concatenate(parts, axis=1) + bfull_ref[...]
        ya_ref[rows, :] = (z_ref[rows, ucols] * s).astype(BF16)

    lbp = lbp_ref[...]
    lmax = jnp.maximum(lbp[0:1], lbp[1:2])
    e0 = jnp.exp(lbp[0:1] - lmax)
    e1 = jnp.exp(lbp[1:2] - lmax)
    lb = e0 / (e0 + e1)

    blk = 2 * HG_CHUNK
    brow = lax.broadcasted_iota(jnp.int32, (blk, blk), 0)
    bcol = lax.broadcasted_iota(jnp.int32, (blk, blk), 1)
    causal = jnp.logical_and(brow >= bcol, (brow < HG_CHUNK) == (bcol < HG_CHUNK))
    tri = jnp.where(causal, 1.0, 0.0).astype(BF16)
    first_chunk = lax.broadcasted_iota(jnp.int32, (blk, HG_KW), 0) < HG_CHUNK
    q0 = 2 * GM_WIDTH
    n_chunks = ts // HG_CHUNK

    n_rb = ts // blk
    qcols = slice(q0, q0 + HG_KW)
    fcols = slice(q0 + HG_KW, q0 + 2 * HG_KW)
    vcols = slice(q0 + 2 * HG_KW, q0 + 3 * HG_KW)
    heads = [slice(hh * HG_KEY, (hh + 1) * HG_KEY) for hh in range(HG_HEADS)]

    def hg_cumsum(rb):
        rows = pl.ds(rb * blk, blk)
        zq = z_ref[rows, qcols]
        f = lb + (1.0 - lb) * jax.nn.sigmoid(z_ref[rows, fcols])
        logf = jnp.log(f)
        lhi = logf.astype(BF16)
        llo = (logf - lhi.astype(F32)).astype(BF16)
        b_ref[rows, :] = _dot(tri, lhi) + _dot(tri, llo)
        z_ref[rows, qcols] = zq * jax.nn.sigmoid(zq)
        z_ref[rows, fcols] = 1.0 - f

    def hg_decay(rb):
        rows = pl.ds(rb * blk, blk)
        b = b_ref[rows, :]
        qf = z_ref[rows, qcols]
        k = z_ref[rows, fcols]
        b_mid = b[HG_CHUNK - 1:HG_CHUNK, :]
        b_end = b[blk - 1:blk, :]
        b_last = jnp.where(first_chunk, b_mid, b_end)
        qd_ref[rows, :] = (qf * jnp.exp(b)).astype(BF16)
        ki_ref[rows, :] = (k * jnp.exp(-b)).astype(BF16)
        ke_ref[rows, :] = (k * jnp.exp(b_last - b)).astype(BF16)
        dec_ref[2 * rb:2 * rb + 1, :] = jnp.exp(b_mid)
        dec_ref[2 * rb + 1:2 * rb + 2, :] = jnp.exp(b_end)

    def hg_scores(rb):
        rows = pl.ds(rb * blk, blk)
        for hh, sl in enumerate(heads):
            att = _dot_nt(qd_ref[rows, sl], ki_ref[rows, sl])
            att_ref[rb, hh] = jnp.where(causal, att, 0.0).astype(BF16)

    def hg_values(rb):
        rows = pl.ds(rb * blk, blk)
        vb = z_ref[rows, vcols].astype(BF16)
        for hh, sl in enumerate(heads):
            oi_ref[rows, sl] = _dot(att_ref[rb, hh], vb[:, sl])
        for cc in range(2):
            crows = pl.ds(rb * blk + cc * HG_CHUNK, HG_CHUNK)
            for hh, sl in enumerate(heads):
                ds_ref[2 * rb + cc, hh] = _dot_tn(
                    vb[cc * HG_CHUNK:(cc + 1) * HG_CHUNK, sl], ke_ref[crows, sl])

    def hg_scan(hh):
        sl = slice(hh * HG_KEY, (hh + 1) * HG_KEY)
        st = st_ref[hh]
        for c in range(n_chunks):
            sc_ref[c, hh] = st.astype(BF16)
            st = st * dec_ref[c:c + 1, sl] + ds_ref[c, hh]
        st_ref[hh] = st

    def hg_out(c):
        rows = pl.ds(c * HG_CHUNK, HG_CHUNK)
        og = jax.nn.sigmoid(z_ref[rows, q0 + 3 * HG_KW:q0 + 4 * HG_KW])
        outs = []
        for hh in range(HG_HEADS):
            sl = slice(hh * HG_KEY, (hh + 1) * HG_KEY)
            o = oi_ref[rows, sl] + _dot_nt(qd_ref[rows, sl], sc_ref[c, hh])
            o = o * lax.rsqrt(jnp.mean(o * o, axis=-1, keepdims=True) + EPS) * ng_ref[:, sl]
            outs.append(o * og[:, sl])
        yb_ref[rows, :] = jnp.concatenate(outs, axis=1).astype(BF16)

    n_gm = ts // GM_CHUNK
    in_proj(q0)
    in_proj(q0 + 2 * HG_KW)
    for rb in range(n_rb):
        hg_cumsum(rb)
    in_proj(0)
    for rb in range(n_rb):
        hg_decay(rb)
    for rb in range(n_rb):
        hg_scores(rb)
    in_proj(q0 + 4 * HG_KW)
    for c in range(n_gm):
        gm_norm(c)
    for rb in range(n_rb):
        hg_values(rb)
    in_proj(q0 + 4 * HG_KW + 1024)
    for c in range(n_gm):
        gm_gate(c)
    for hh in range(HG_HEADS):
        hg_scan(hh)
    for c in range(n_chunks):
        hg_out(c)

    ga0 = q0 + 4 * HG_KW
    gb0 = ga0 + x_ref.shape[2]
    for n in range(0, x_ref.shape[2], 512):
        ua = _dot(ya_ref[...], wupa_ref[:, n:n + 512])
        ub = _dot(yb_ref[...], wupb_ref[:, n:n + 512])
        merged = (jax.nn.sigmoid(z_ref[:, ga0 + n:ga0 + n + 512]) * ua
                  + jax.nn.sigmoid(z_ref[:, gb0 + n:gb0 + n + 512]) * ub)
        m_ref[:, n:n + 512] = merged.astype(BF16)
    x1 = x_ref[0] + _dot(m_ref[...], wout_ref[...])
    x1_ref[0] = x1
    h2 = _rms(x1, gffn_ref[...])
    _store_token_rows(h2_ref, h2)
    hi = h2.astype(BF16)
    lo = (h2 - hi.astype(F32)).astype(BF16)
    l2 = _dot(hi, wrt2_ref[...])
    lg_ref[0] = l2[:, :ROUTE_COLS] + l2[:, ROUTE_COLS:] + _dot(lo, wrt1_ref[...])


def _mixer(x, g_mix, w_in, ln_g, ln_b, w_sp, bfull, w_up_a, lbp, norm_g, w_up_b, w_out, g_ffn,
           w_rt2, w_rt1):
    bsz, seq, d = x.shape
    ts = min(MIX_TILE, seq)
    in_cols = w_in.shape[1]

    def full(a):
        nd = a.ndim
        return pl.BlockSpec(a.shape, lambda b, j, _nd=nd: (0,) * _nd, pipeline_mode=pl.Buffered(1))

    consts = (g_mix, w_in, ln_g, ln_b, w_sp, bfull, w_up_a, lbp, norm_g, w_up_b, w_out, g_ffn,
              w_rt2, w_rt1)
    assert d == ROW_TILE * LANES
    nj = seq // ts
    tile = lambda w: pl.BlockSpec((1, ts, w), lambda b, j: (b, j, 0))
    return pl.pallas_call(
        _mixer_kernel,
        grid=(bsz, nj),
        in_specs=[tile(d)] + [full(a) for a in consts],
        out_specs=[tile(d),
                   pl.BlockSpec((ts * ROW_TILE, LANES), lambda b, j: (b * nj + j, 0)),
                   tile(ROUTE_COLS)],
        out_shape=[jax.ShapeDtypeStruct((bsz, seq, d), F32),
                   jax.ShapeDtypeStruct((bsz * seq * ROW_TILE, LANES), F32),
                   jax.ShapeDtypeStruct((bsz, seq, ROUTE_COLS), F32)],
        scratch_shapes=[pltpu.VMEM((ts, in_cols), F32),
                        pltpu.VMEM((ts, d), BF16),
                        pltpu.VMEM((ts, GM_WIDTH), BF16),
                        pltpu.VMEM((ts, HG_KW), BF16),
                        pltpu.VMEM((ts, d), BF16),
                        pltpu.VMEM((HG_HEADS, HG_KEY, HG_KEY), F32),
                        pltpu.VMEM((ts, HG_KW), BF16),
                        pltpu.VMEM((ts, HG_KW), F32),
                        pltpu.VMEM((ts // HG_CHUNK, HG_HEADS, HG_KEY, HG_KEY), F32),
                        pltpu.VMEM((ts // HG_CHUNK, HG_HEADS, HG_KEY, HG_KEY), BF16),
                        pltpu.VMEM((ts // HG_CHUNK, HG_KW), F32),
                        pltpu.VMEM((ts, HG_KW), F32),
                        pltpu.VMEM((ts, HG_KW), BF16),
                        pltpu.VMEM((ts, HG_KW), BF16),
                        pltpu.VMEM((ts // (2 * HG_CHUNK), HG_HEADS, 2 * HG_CHUNK, 2 * HG_CHUNK),
                                   BF16)],
        compiler_params=pltpu.CompilerParams(
            dimension_semantics=("arbitrary", "arbitrary"), vmem_limit_bytes=VMEM_LIMIT),
        name="mixer",
    )(x, *consts)


def _route_kernel(lg_ref, utri_ref, dest_ref, gt_ref, binfo_ref, einfo_ref, rec_ref, cnt_ref):
    phase = pl.program_id(0)
    i = pl.program_id(1)
    r = utri_ref.shape[0]
    n_sub = lg_ref.shape[0] // r
    sub8 = lax.broadcasted_iota(jnp.int32, (8, r), 0)
    e32 = lax.broadcasted_iota(jnp.int32, (N_EXPERTS, r), 0).astype(F32)

    @pl.when(jnp.logical_and(phase == 0, i == 0))
    def _():
        cnt_ref[...] = jnp.zeros_like(cnt_ref)

    def rank_tile(s):
        lt = lg_ref[s * r:(s + 1) * r, :].T
        l0, l1, l2, l3 = lt[0:1], lt[1:2], lt[2:3], lt[3:4]
        gmax = jnp.maximum(jnp.maximum(l0, l1), jnp.maximum(l2, l3))
        gsum = (jnp.exp(l0 - gmax) + jnp.exp(l1 - gmax)) + (jnp.exp(l2 - gmax) + jnp.exp(l3 - gmax))
        p_g = 1.0 / gsum
        gsel = jnp.where(l0 == gmax, 0, jnp.where(l1 == gmax, 1, jnp.where(l2 == gmax, 2, 3)))
        eg = [lt[EXP_ROW0 + EXP_PER_GROUP * g:EXP_ROW0 + EXP_PER_GROUP * (g + 1)]
              for g in range(N_GROUPS)]
        sel = jnp.where(gsel == 0, eg[0], jnp.where(gsel == 1, eg[1],
                                                     jnp.where(gsel == 2, eg[2], eg[3])))
        m1 = jnp.max(sel, axis=0, keepdims=True)
        i1 = jnp.min(jnp.where(sel == m1, sub8, EXP_PER_GROUP), axis=0, keepdims=True)
        sel2 = jnp.where(sub8 == i1, -jnp.inf, sel)
        m2 = jnp.max(sel2, axis=0, keepdims=True)
        i2 = jnp.min(jnp.where(sel2 == m2, sub8, EXP_PER_GROUP), axis=0, keepdims=True)
        ex = jnp.exp(m2 - m1)
        g1 = 1.0 / (1.0 + ex)
        g2 = ex * g1
        eid1 = (gsel * EXP_PER_GROUP + i1).astype(F32)
        eid2 = (gsel * EXP_PER_GROUP + i2).astype(F32)
        oh1 = jnp.where(e32 == eid1, 1.0, 0.0)
        oh2 = jnp.where(e32 == eid2, 1.0, 0.0)
        cum1 = _dot(oh1.astype(BF16), utri_ref[...])
        cum2 = _dot(oh2.astype(BF16), utri_ref[...])
        tot1 = jnp.sum(oh1, axis=1, keepdims=True)
        tot2 = jnp.sum(oh2, axis=1, keepdims=True)
        base = cnt_ref[:, 0:1]
        rank1 = jnp.sum(oh1 * (base + cum1), axis=0, keepdims=True)
        rank2 = jnp.sum(oh2 * (base + tot1 + cum2), axis=0, keepdims=True)
        cnt_ref[...] = cnt_ref[...] + (tot1 + tot2)
        rec = jnp.where(sub8 == 0, eid1, jnp.where(sub8 == 1, eid2, jnp.where(
            sub8 == 2, rank1, jnp.where(sub8 == 3, rank2, jnp.where(
                sub8 == 4, g1 * p_g, jnp.where(sub8 == 5, g2 * p_g, 0.0))))))
        rec_ref[i * n_sub + s] = rec

    @pl.when(phase == 0)
    def _():
        for s in range(n_sub):
            rank_tile(s)

    @pl.when(phase == 1)
    def _():
        cnt = cnt_ref[...]
        padded = jnp.floor((cnt + (FFN_BLOCK - 1)) * (1.0 / FFN_BLOCK)) * FFN_BLOCK
        rows = lax.broadcasted_iota(jnp.int32, cnt.shape, 0)
        pad_end = padded
        for s in (1, 2, 4, 8, 16):
            pad_end = pad_end + jnp.where(rows >= s, pltpu.roll(pad_end, s, axis=0), 0.0)
        pad_start = (pad_end - padded)[:, 0:1]
        subw = lax.broadcasted_iota(jnp.int32, (ROUTE_COLS, r), 0)
        for s in range(n_sub):
            rec = rec_ref[i * n_sub + s]
            d1 = rec[2:3] + jnp.sum(jnp.where(e32 == rec[0:1], pad_start, 0.0), axis=0,
                                    keepdims=True)
            d2 = rec[3:4] + jnp.sum(jnp.where(e32 == rec[1:2], pad_start, 0.0), axis=0,
                                    keepdims=True)
            dest_ref[s] = jnp.where(sub8 == 0, d1, jnp.where(sub8 == 1, d2, 0.0)).astype(jnp.int32)
            gates = jnp.where(subw == 0, rec[4:5], jnp.where(subw == 1, rec[5:6], 0.0))
            gt_ref[s * r:(s + 1) * r, :] = gates.T
        nbp = binfo_ref.shape[1]
        blk_start = (lax.broadcasted_iota(jnp.int32, (N_EXPERTS, nbp), 1) * FFN_BLOCK).astype(F32)
        n_le = jnp.sum(jnp.where(pad_end[:, 0:1] <= blk_start, 1.0, 0.0), axis=0, keepdims=True)
        binfo_ref[...] = jnp.broadcast_to(n_le, binfo_ref.shape).astype(jnp.int32)
        lane = lax.broadcasted_iota(jnp.int32, cnt.shape, 1)
        einfo_ref[...] = jnp.where(lane == 0, pad_end - padded + cnt, pad_end).astype(jnp.int32)


def _route(logits, nbp):
    t = logits.shape[0]
    r = min(ROUTE_TILE, t)
    rr = min(ROUTE_SUBTILES * r, t)
    nt = t // rr
    utri = jnp.triu(jnp.ones((r, r), F32), k=1).astype(BF16)
    return pl.pallas_call(
        _route_kernel,
        grid=(2, nt),
        in_specs=[pl.BlockSpec((rr, ROUTE_COLS), lambda p, i: (i * (1 - p) + (nt - 1) * p, 0)),
                  pl.BlockSpec((r, r), lambda p, i: (0, 0))],
        out_specs=[pl.BlockSpec((rr // r, 8, r), lambda p, i: (i * p, 0, 0)),
                   pl.BlockSpec((rr, ROUTE_COLS), lambda p, i: (i * p, 0)),
                   pl.BlockSpec((8, nbp), lambda p, i: (0, 0)),
                   pl.BlockSpec((N_EXPERTS, LANES), lambda p, i: (0, 0))],
        out_shape=[jax.ShapeDtypeStruct((t // r, 8, r), jnp.int32),
                   jax.ShapeDtypeStruct((t, ROUTE_COLS), F32),
                   jax.ShapeDtypeStruct((8, nbp), jnp.int32),
                   jax.ShapeDtypeStruct((N_EXPERTS, LANES), jnp.int32)],
        scratch_shapes=[pltpu.VMEM((t // r, 8, r), F32),
                        pltpu.VMEM((N_EXPERTS, LANES), F32)],
        compiler_params=pltpu.CompilerParams(
            dimension_semantics=("arbitrary", "arbitrary"), vmem_limit_bytes=VMEM_LIMIT),
        name="route",
    )(logits, utri)


DMA_UNROLL = 8


def _token_copy(src, src_tok, dst, dst_tok, sem):
    return pltpu.make_async_copy(
        src.at[pl.ds(pl.multiple_of(src_tok * ROW_TILE, ROW_TILE), ROW_TILE)],
        dst.at[pl.ds(pl.multiple_of(dst_tok * ROW_TILE, ROW_TILE), ROW_TILE)], sem)


def _pad_copies(plo_ref, phi_ref, zero_ref, buf_hbm, sem):
    copies = []
    for e in range(N_EXPERTS):
        lo = plo_ref[e]
        n = phi_ref[e] - lo
        bit = FFN_BLOCK // 2
        while bit >= 1:
            first = lo + (n & ~(2 * bit - 1))
            copies.append(((n & bit) != 0, pltpu.make_async_copy(
                zero_ref.at[pl.ds(0, bit * ROW_TILE)],
                buf_hbm.at[pl.ds(pl.multiple_of(first * ROW_TILE, ROW_TILE), bit * ROW_TILE)],
                sem)))
            bit //= 2
    blk_rows = FFN_BLOCK * ROW_TILE
    n_blocks = buf_hbm.shape[0] // blk_rows
    n_used = phi_ref[N_EXPERTS - 1] // FFN_BLOCK
    for j in range(N_EXPERTS):
        for half in range(2):
            row0 = pl.multiple_of((n_used + j) * blk_rows + half * (blk_rows // 2), ROW_TILE)
            copies.append((n_used + j < n_blocks, pltpu.make_async_copy(
                zero_ref, buf_hbm.at[pl.ds(row0, blk_rows // 2)], sem)))
    return copies


def _scatter_kernel(plo_ref, phi_ref, dest_ref, h2_ref, buf_hbm, zero_ref, sem, zsem):
    r = dest_ref.shape[0] // 2

    @pl.when(pl.program_id(0) == 0)
    def _():
        zero_ref[...] = jnp.zeros_like(zero_ref)
        for cond, cp in _pad_copies(plo_ref, phi_ref, zero_ref, buf_hbm, zsem):
            pl.when(cond)(cp.start)
        for cond, cp in _pad_copies(plo_ref, phi_ref, zero_ref, buf_hbm, zsem):
            pl.when(cond)(cp.wait)

    def issue(g, carry):
        t0 = g * DMA_UNROLL
        idx = [[dest_ref[k * r + t0 + u] for k in range(2)] for u in range(DMA_UNROLL)]
        for u in range(DMA_UNROLL):
            for k in range(2):
                _token_copy(h2_ref, t0 + u, buf_hbm, idx[u][k], sem).start(priority=k)
        return carry

    lax.fori_loop(0, r // DMA_UNROLL, issue, 0)

    def drain(t, carry):
        _token_copy(h2_ref, 0, buf_hbm, 0, sem).wait()
        return carry

    lax.fori_loop(0, 2 * r, drain, 0, unroll=DMA_UNROLL)


def _scatter(pad_lo, pad_hi, dest, r, h2, n_rows):
    nt = dest.shape[0] // (2 * r)
    return pl.pallas_call(
        _scatter_kernel,
        grid_spec=pltpu.PrefetchScalarGridSpec(
            num_scalar_prefetch=2,
            grid=(nt,),
            in_specs=[pl.BlockSpec((2 * r,), lambda i, lo, hi: (i,), memory_space=pltpu.SMEM),
                      pl.BlockSpec((r * ROW_TILE, LANES), lambda i, lo, hi: (i, 0))],
            out_specs=pl.BlockSpec(memory_space=pl.ANY),
            scratch_shapes=[pltpu.VMEM((FFN_BLOCK // 2 * ROW_TILE, LANES), F32),
                            pltpu.SemaphoreType.DMA(()),
                            pltpu.SemaphoreType.DMA(())]),
        out_shape=jax.ShapeDtypeStruct((n_rows * ROW_TILE, LANES), F32),
        compiler_params=pltpu.CompilerParams(
            dimension_semantics=("arbitrary",), has_side_effects=True),
        name="scatter",
    )(pad_lo, pad_hi, dest, h2)


FFN_SLOTS = 3


def _ffn_kernel(be_ref, nu_ref, ne_ref, run_ref, x_hbm, w1_hbm, w3_hbm, w2_hbm, y_ref,
                w13_s, w2_s, x_s, w1_f, w3_f, w2_f, xsems, wsems):
    i = pl.program_id(0)
    blk_rows = FFN_BLOCK * ROW_TILE

    def weight_copies(expert, slot):
        return [pltpu.make_async_copy(src.at[expert], dst.at[slot], wsems.at[slot])
                for src, dst in ((w1_hbm, w1_f), (w3_hbm, w3_f), (w2_hbm, w2_f))]

    def block_copy(blk):
        src = jnp.minimum(blk, nu_ref[0] - 1)
        return pltpu.make_async_copy(
            x_hbm.at[pl.ds(pl.multiple_of(src * blk_rows, blk_rows), blk_rows)],
            x_s.at[blk % FFN_SLOTS], xsems.at[blk % FFN_SLOTS])

    @pl.when(i == 0)
    def _():
        for cp in weight_copies(be_ref[0], 0):
            cp.start()
        block_copy(0).start()
        block_copy(1).start()

    block_copy(i + 2).start()
    block_copy(i).wait()
    x_ref = x_s.at[i % FFN_SLOTS]

    @pl.when(i + 1 == pl.num_programs(0))
    def _():
        block_copy(i + 1).wait()
        block_copy(i + 2).wait()

    e = be_ref[i]
    used = e < N_EXPERTS
    fresh = jnp.logical_or(i == 0, e != be_ref[jnp.maximum(i - 1, 0)])

    half = D_EXPERT // 2
    wslot = run_ref[i] % 2

    @pl.when(jnp.logical_and(used, fresh))
    def _():
        for cp in weight_copies(e, wslot):
            cp.wait()
        for c in range(2):
            w13_s[c, :, :half] = w1_f[wslot, :, c * half:(c + 1) * half].astype(BF16)
            w13_s[c, :, half:] = w3_f[wslot, :, c * half:(c + 1) * half].astype(BF16)
        w2_s[...] = w2_f[wslot].astype(BF16)
        nxt = ne_ref[i]

        @pl.when(nxt < N_EXPERTS)
        def _():
            for cp in weight_copies(nxt, 1 - wslot):
                cp.start()

    @pl.when(used)
    def _():
        kw = 2 * LANES
        h = [None, None]
        for kc in range(w13_s.shape[1] // kw):
            xk = jnp.concatenate(
                [x_ref[pl.ds(2 * kc + j, FFN_BLOCK, stride=ROW_TILE), :] for j in range(2)],
                axis=1).astype(BF16)
            for c in range(2):
                part = _dot(xk, w13_s[c, kc * kw:(kc + 1) * kw, :])
                h[c] = part if h[c] is None else h[c] + part
        acts = []
        for c in range(2):
            a = h[c][:, :half]
            acts.append((a * jax.nn.sigmoid(a) * h[c][:, half:]).astype(BF16))
        for nc in range(y_ref.shape[0] // FFN_BLOCK // 2):
            cols = slice(nc * kw, (nc + 1) * kw)
            y = (_dot(acts[0], w2_s[0:half, cols]) + _dot(acts[1], w2_s[half:2 * half, cols]))
            for j in range(2):
                y_ref[pl.ds(2 * nc + j, FFN_BLOCK, stride=ROW_TILE), :] = (
                    y[:, j * LANES:(j + 1) * LANES])

    @pl.when(jnp.logical_not(used))
    def _():
        y_ref[...] = jnp.zeros_like(y_ref)


def _ffn(blk_e, n_used, buf, w1, w3, w2):
    nb = buf.shape[0] // (FFN_BLOCK * ROW_TILE)
    d = w1.shape[1]
    run = jnp.cumsum(jnp.concatenate(
        [jnp.zeros((1,), jnp.int32), (blk_e[1:] != blk_e[:-1]).astype(jnp.int32)]))
    after = jnp.searchsorted(blk_e, blk_e, side="right").astype(jnp.int32)
    next_e = jnp.where(after < nb, blk_e[jnp.minimum(after, nb - 1)], N_EXPERTS).astype(jnp.int32)
    rows = pl.BlockSpec((FFN_BLOCK * ROW_TILE, LANES), lambda i, be, nu, ne, rn: (i, 0))
    hbm = pl.BlockSpec(memory_space=pl.ANY)
    return pl.pallas_call(
        _ffn_kernel,
        grid_spec=pltpu.PrefetchScalarGridSpec(
            num_scalar_prefetch=4,
            grid=(nb,),
            in_specs=[hbm, hbm, hbm, hbm],
            out_specs=rows,
            scratch_shapes=[pltpu.VMEM((2, d, D_EXPERT), BF16),
                            pltpu.VMEM((D_EXPERT, d), BF16),
                            pltpu.VMEM((FFN_SLOTS, FFN_BLOCK * ROW_TILE, LANES), F32),
                            pltpu.VMEM((2, d, D_EXPERT), F32),
                            pltpu.VMEM((2, d, D_EXPERT), F32),
                            pltpu.VMEM((2, D_EXPERT, d), F32),
                            pltpu.SemaphoreType.DMA((FFN_SLOTS,)),
                            pltpu.SemaphoreType.DMA((2,))]),
        out_shape=jax.ShapeDtypeStruct(buf.shape, F32),
        compiler_params=pltpu.CompilerParams(
            dimension_semantics=("arbitrary",), vmem_limit_bytes=VMEM_LIMIT),
        name="ffn",
    )(blk_e, n_used, next_e, run, buf, w1, w3, w2)


GATHER_SLOTS = 3


def _final_kernel(dest_ref, dn1_ref, dn2_ref, x1_ref, p_ref, gt_ref, ybuf_hbm, gple_ref, wpg_ref,
                  wple_ref, gfin_ref, out_ref, y_ref, x2_ref, h3_ref, g_ref, pb_ref, sems):
    i = pl.program_id(0)
    r = x1_ref.shape[0]
    slot = i % GATHER_SLOTS
    ahead = (i + GATHER_SLOTS - 1) % GATHER_SLOTS

    def issue_group(d_ref, s, t0):
        idx = [[d_ref[k * r + t0 + u] for k in range(2)] for u in range(DMA_UNROLL)]
        for u in range(DMA_UNROLL):
            for k in range(2):
                _token_copy(ybuf_hbm, idx[u][k], y_ref.at[s, k], t0 + u,
                            sems.at[s]).start(priority=k)

    def drain(s):
        def body(t, carry):
            _token_copy(ybuf_hbm, 0, y_ref.at[s, 0], 0, sems.at[s]).wait()
            return carry

        lax.fori_loop(0, 2 * r, body, 0, unroll=DMA_UNROLL)

    @pl.when(i == 0)
    def _():
        def body(g, carry):
            issue_group(dest_ref, 0, g * DMA_UNROLL)
            issue_group(dn1_ref, 1, g * DMA_UNROLL)
            return carry

        lax.fori_loop(0, r // DMA_UNROLL, body, 0)

    drain(slot)

    n_batch = 4
    per = r // n_batch

    def issue_batch(b):
        for t0 in range(b * per, (b + 1) * per, DMA_UNROLL):
            issue_group(dn2_ref, ahead, t0)

    n_chunk = 8
    rc = r // n_chunk
    issue_batch(0)
    for c in range(n_chunk):
        rows = pl.ds(c * rc, rc)
        x2 = (x1_ref[rows, :]
              + gt_ref[rows, 0:1] * _load_token_rows(y_ref.at[slot, 0], rc, c * rc)
              + gt_ref[rows, 1:2] * _load_token_rows(y_ref.at[slot, 1], rc, c * rc))
        x2_ref[rows, :] = x2
        h3_ref[rows, :] = _rms(x2, gple_ref[...]).astype(BF16)
    issue_batch(1)
    pb_ref[...] = p_ref[...].astype(BF16)
    for n in range(0, x2_ref.shape[1], 256):
        cols = slice(n, n + 256)
        g_ref[:, cols] = (jax.nn.sigmoid(_dot(h3_ref[...], wpg_ref[:, cols]))
                          * _dot(pb_ref[...], wple_ref[:, cols]))
        if n == 256:
            issue_batch(2)
    issue_batch(3)
    for c in range(n_chunk):
        rows = pl.ds(c * rc, rc)
        out_ref[rows, :] = _rms(x2_ref[rows, :] + g_ref[rows, :], gfin_ref[...])

    @pl.when(i + 1 == pl.num_programs(0))
    def _():
        for other in range(1, GATHER_SLOTS):
            drain((i + other) % GATHER_SLOTS)


def _final(dest, r, x1, p, gt, ybuf, g_ple, w_pg, w_ple, g_final):
    t, d = x1.shape
    nt = t // r

    def full(a):
        nd = a.ndim
        return pl.BlockSpec(a.shape, lambda i, _nd=nd: (0,) * _nd, pipeline_mode=pl.Buffered(1))

    return pl.pallas_call(
        _final_kernel,
        grid=(nt,),
        in_specs=[pl.BlockSpec((2 * r,), lambda i: (i,), memory_space=pltpu.SMEM),
                  pl.BlockSpec((2 * r,), lambda i: (jnp.minimum(i + 1, nt - 1),),
                               memory_space=pltpu.SMEM),
                  pl.BlockSpec((2 * r,), lambda i: (jnp.minimum(i + 2, nt - 1),),
                               memory_space=pltpu.SMEM),
                  pl.BlockSpec((r, d), lambda i: (i, 0)),
                  pl.BlockSpec((r, p.shape[1]), lambda i: (i, 0)),
                  pl.BlockSpec((r, ROUTE_COLS), lambda i: (i, 0)),
                  pl.BlockSpec(memory_space=pl.ANY),
                  full(g_ple), full(w_pg), full(w_ple), full(g_final)],
        out_specs=pl.BlockSpec((r, d), lambda i: (i, 0)),
        out_shape=jax.ShapeDtypeStruct((t, d), F32),
        scratch_shapes=[pltpu.VMEM((GATHER_SLOTS, 2, r * ROW_TILE, LANES), F32),
                        pltpu.VMEM((r, d), F32),
                        pltpu.VMEM((r, d), BF16),
                        pltpu.VMEM((r, d), F32),
                        pltpu.VMEM((r, p.shape[1]), BF16),
                        pltpu.SemaphoreType.DMA((GATHER_SLOTS,))],
        compiler_params=pltpu.CompilerParams(
            dimension_semantics=("arbitrary",), vmem_limit_bytes=VMEM_LIMIT),
        name="final",
    )(dest, dest, dest, x1, p, gt, ybuf, g_ple, w_pg, w_ple, g_final)


def kernel(x, p, g_mix, w_in, gm_ln_g, gm_ln_b, gm_w_sp, gm_b_sp, w_up_a, hg_lb_param, hg_norm_g,
           w_up_b, w_out, g_ffn, w_grp, w_exp, w1, w3, w2, g_ple, w_pg, w_ple, g_final):
    bsz, seq, d = x.shape
    t = bsz * seq
    row = lambda a: a.reshape(1, -1)

    assert w_in.shape[0] == 1 and hg_lb_param.shape[0] == 2, "single-layer block"
    i = 0
    w_rt = jnp.zeros((d, ROUTE_COLS), F32)
    w_rt = w_rt.at[:, 0:N_GROUPS].set(w_grp[i])
    w_rt = w_rt.at[:, EXP_ROW0:EXP_ROW0 + N_EXPERTS].set(w_exp[i])
    w_rt_hi = w_rt.astype(BF16)
    w_rt_lo = (w_rt - w_rt_hi.astype(F32)).astype(BF16)
    w_rt2 = jnp.concatenate([w_rt_hi, w_rt_lo], axis=1)
    bfull = jnp.repeat(gm_b_sp[i].T, GM_WIDTH // GM_GROUPS, axis=1)

    x1, h2, logits = _mixer(
        x, row(g_mix[i]), w_in[i].astype(BF16), row(gm_ln_g[i]), row(gm_ln_b[i]), gm_w_sp[i],
        bfull, w_up_a[i].astype(BF16), hg_lb_param, row(hg_norm_g[i]),
        w_up_b[i].astype(BF16), w_out[i].astype(BF16), row(g_ffn[i]), w_rt2, w_rt_hi)

    n_rows = 2 * t + N_EXPERTS * FFN_BLOCK
    nb = n_rows // FFN_BLOCK
    nbp = -(-nb // LANES) * LANES
    dest, gt, binfo, einfo = _route(logits.reshape(t, ROUTE_COLS), nbp)
    r = dest.shape[2]
    dest = dest[:, 0:2, :].reshape(-1)
    buf = _scatter(einfo[:, 0], einfo[:, 1], dest, r, h2, n_rows)
    n_used = einfo[N_EXPERTS - 1:, 1] // FFN_BLOCK
    ybuf = _ffn(binfo[0, :nb], n_used, buf, w1[i], w3[i], w2[i])
    out = _final(dest, r, x1.reshape(t, d), p[i].reshape(t, -1), gt, ybuf, row(g_ple[i]),
                 w_pg[i].astype(BF16), w_ple[i].astype(BF16), row(g_final))
    return out.reshape(bsz, seq, d)
```

```python
import functools

import jax
import jax.numpy as jnp
from jax import lax
from jax.experimental import pallas as pl
from jax.experimental.pallas import tpu as pltpu

F32 = jnp.float32
BF16 = jnp.bfloat16

EPS = 1e-6
GM_WIDTH = 512
GM_GROUPS = 8
GM_CHUNK = 128
HG_HEADS = 4
HG_KEY = 128
HG_CHUNK = 64
HG_KW = HG_HEADS * HG_KEY
N_GROUPS = 4
EXP_PER_GROUP = 8
N_EXPERTS = N_GROUPS * EXP_PER_GROUP
D_EXPERT = 512
LANES = 128
ROUTE_COLS = LANES
EXP_ROW0 = 8

MIX_TILE = 512
ROUTE_TILE = 512
ROUTE_SUBTILES = 4
FFN_BLOCK = 512
VMEM_LIMIT = 56 * 1024 * 1024


def _rms(x, g):
    return x * lax.rsqrt(jnp.mean(x * x, axis=-1, keepdims=True) + EPS) * g


def _dot(a, b):
    return jnp.dot(a, b, preferred_element_type=F32)


def _dot_nt(a, b):
    return lax.dot_general(a, b, (((1,), (1,)), ((), ())), preferred_element_type=F32)


def _dot_tn(a, b):
    return lax.dot_general(a, b, (((0,), (0,)), ((), ())), preferred_element_type=F32)


ROW_TILE = 8


def _store_token_rows(ref, val):
    n = val.shape[0]
    for c in range(ROW_TILE):
        ref[pl.ds(c, n, stride=ROW_TILE), :] = val[:, c * LANES:(c + 1) * LANES]


def _load_token_rows(ref, n, first=0):
    return jnp.concatenate(
        [ref[pl.ds(first * ROW_TILE + c, n, stride=ROW_TILE), :] for c in range(ROW_TILE)], axis=1)


def _mixer_kernel(x_ref, gmix_ref, win_ref, lng_ref, lnb_ref, wsp_ref, bfull_ref, wupa_ref,
                  lbp_ref, ng_ref, wupb_ref, wout_ref, gffn_ref, wrt2_ref, wrt1_ref,
                  x1_ref, h2_ref, lg_ref,
                  z_ref, h_ref, ya_ref, yb_ref, m_ref, st_ref, qd_ref, oi_ref, ds_ref, sc_ref,
                  dec_ref, b_ref, ki_ref, ke_ref, att_ref):
    ts = x_ref.shape[1]
    in_cols = win_ref.shape[1]

    @pl.when(pl.program_id(1) == 0)
    def _():
        st_ref[...] = jnp.zeros_like(st_ref)

    h_ref[...] = _rms(x_ref[0], gmix_ref[...]).astype(BF16)
    def in_proj(col0):
        for n in range(col0, col0 + 1024, 512):
            z_ref[:, n:n + 512] = _dot(h_ref[...], win_ref[:, n:n + 512])

    row = lax.broadcasted_iota(jnp.int32, (GM_CHUNK, GM_CHUNK), 0)
    col = lax.broadcasted_iota(jnp.int32, (GM_CHUNK, GM_CHUNK), 1)
    tril = row >= col
    low_half = col < (LANES // 2)
    ucols = slice(0, GM_WIDTH)
    ncols = slice(GM_WIDTH, 2 * GM_WIDTH)

    def gm_norm(c):
        rows = pl.ds(c * GM_CHUNK, GM_CHUNK)
        z_ref[rows, ucols] = jax.nn.gelu(z_ref[rows, ucols])
        v = jax.nn.gelu(z_ref[rows, ncols])
        d = v - jnp.mean(v, axis=-1, keepdims=True)
        var = jnp.mean(d * d, axis=-1, keepdims=True)
        z_ref[rows, ncols] = d * lax.rsqrt(var + EPS) * lng_ref[...] + lnb_ref[...]

    def gm_gate(c):
        rows = pl.ds(c * GM_CHUNK, GM_CHUNK)
        parts = []
        for p in range(GM_GROUPS // 2):
            vp = z_ref[rows, GM_WIDTH + p * LANES:GM_WIDTH + (p + 1) * LANES]
            v_lo = jnp.where(low_half, vp, 0.0).astype(BF16)
            v_hi = jnp.where(low_half, 0.0, vp).astype(BF16)
            w_lo = jnp.where(tril, wsp_ref[2 * p], 0.0).astype(BF16)
            w_hi = jnp.where(tril, wsp_ref[2 * p + 1], 0.0).astype(BF16)
            parts.append(_dot(w_lo, v_lo) + _dot(w_hi, v_hi))
        s = jnp.concatenate(parts, axis=1) + bfull_ref[...]
        ya_ref[rows, :] = (z_ref[rows, ucols] * s).astype(BF16)

    lbp = lbp_ref[...]
    lmax = jnp.maximum(lbp[0:1], lbp[1:2])
    e0 = jnp.exp(lbp[0:1] - lmax)
    e1 = jnp.exp(lbp[1:2] - lmax)
    lb = e0 / (e0 + e1)

    blk = 2 * HG_CHUNK
    brow = lax.broadcasted_iota(jnp.int32, (blk, blk), 0)
    bcol = lax.broadcasted_iota(jnp.int32, (blk, blk), 1)
    causal = jnp.logical_and(brow >= bcol, (brow < HG_CHUNK) == (bcol < HG_CHUNK))
    tri = jnp.where(causal, 1.0, 0.0).astype(BF16)
    first_chunk = lax.broadcasted_iota(jnp.int32, (blk, HG_KW), 0) < HG_CHUNK
    q0 = 2 * GM_WIDTH
    n_chunks = ts // HG_CHUNK

    n_rb = ts // blk
    qcols = slice(q0, q0 + HG_KW)
    fcols = slice(q0 + HG_KW, q0 + 2 * HG_KW)
    vcols = slice(q0 + 2 * HG_KW, q0 + 3 * HG_KW)
    heads = [slice(hh * HG_KEY, (hh + 1) * HG_KEY) for hh in range(HG_HEADS)]

    def hg_cumsum(rb):
        rows = pl.ds(rb * blk, blk)
        zq = z_ref[rows, qcols]
        f = lb + (1.0 - lb) * jax.nn.sigmoid(z_ref[rows, fcols])
        logf = jnp.log(f)
        lhi = logf.astype(BF16)
        llo = (logf - lhi.astype(F32)).astype(BF16)
        b_ref[rows, :] = _dot(tri, lhi) + _dot(tri, llo)
        z_ref[rows, qcols] = zq * jax.nn.sigmoid(zq)
        z_ref[rows, fcols] = 1.0 - f

    def hg_decay(rb):
        rows = pl.ds(rb * blk, blk)
        b = b_ref[rows, :]
        qf = z_ref[rows, qcols]
        k = z_ref[rows, fcols]
        b_mid = b[HG_CHUNK - 1:HG_CHUNK, :]
        b_end = b[blk - 1:blk, :]
        b_last = jnp.where(first_chunk, b_mid, b_end)
        qd_ref[rows, :] = (qf * jnp.exp(b)).astype(BF16)
        ki_ref[rows, :] = (k * jnp.exp(-b)).astype(BF16)
        ke_ref[rows, :] = (k * jnp.exp(b_last - b)).astype(BF16)
        dec_ref[2 * rb:2 * rb + 1, :] = jnp.exp(b_mid)
        dec_ref[2 * rb + 1:2 * rb + 2, :] = jnp.exp(b_end)

    def hg_scores(rb):
        rows = pl.ds(rb * blk, blk)
        for hh, sl in enumerate(heads):
            att = _dot_nt(qd_ref[rows, sl], ki_ref[rows, sl])
            att_ref[rb, hh] = jnp.where(causal, att, 0.0).astype(BF16)

    def hg_values(rb):
        rows = pl.ds(rb * blk, blk)
        vb = z_ref[rows, vcols].astype(BF16)
        for hh, sl in enumerate(heads):
            oi_ref[rows, sl] = _dot(att_ref[rb, hh], vb[:, sl])
        for cc in range(2):
            crows = pl.ds(rb * blk + cc * HG_CHUNK, HG_CHUNK)
            for hh, sl in enumerate(heads):
                ds_ref[2 * rb + cc, hh] = _dot_tn(
                    vb[cc * HG_CHUNK:(cc + 1) * HG_CHUNK, sl], ke_ref[crows, sl])

    def hg_scan(hh):
        sl = slice(hh * HG_KEY, (hh + 1) * HG_KEY)
        st = st_ref[hh]
        for c in range(n_chunks):
            sc_ref[c, hh] = st.astype(BF16)
            st = st * dec_ref[c:c + 1, sl] + ds_ref[c, hh]
        st_ref[hh] = st

    def hg_out(c):
        rows = pl.ds(c * HG_CHUNK, HG_CHUNK)
        og = jax.nn.sigmoid(z_ref[rows, q0 + 3 * HG_KW:q0 + 4 * HG_KW])
        outs = []
        for hh in range(HG_HEADS):
            sl = slice(hh * HG_KEY, (hh + 1) * HG_KEY)
            o = oi_ref[rows, sl] + _dot_nt(qd_ref[rows, sl], sc_ref[c, hh])
            o = o * lax.rsqrt(jnp.mean(o * o, axis=-1, keepdims=True) + EPS) * ng_ref[:, sl]
            outs.append(o * og[:, sl])
        yb_ref[rows, :] = jnp.concatenate(outs, axis=1).astype(BF16)

    n_gm = ts // GM_CHUNK
    in_proj(q0)
    in_proj(q0 + 2 * HG_KW)
    for rb in range(n_rb):
        hg_cumsum(rb)
    in_proj(0)
    for rb in range(n_rb):
        hg_decay(rb)
    for rb in range(n_rb):
        hg_scores(rb)
    in_proj(q0 + 4 * HG_KW)
    for c in range(n_gm):
        gm_norm(c)
    for rb in range(n_rb):
        hg_values(rb)
    in_proj(q0 + 4 * HG_KW + 1024)
    for c in range(n_gm):
        gm_gate(c)
    for hh in range(HG_HEADS):
        hg_scan(hh)
    for c in range(n_chunks):
        hg_out(c)

    ga0 = q0 + 4 * HG_KW
    gb0 = ga0 + x_ref.shape[2]
    for n in range(0, x_ref.shape[2], 512):
        ua = _dot(ya_ref[...], wupa_ref[:, n:n + 512])
        ub = _dot(yb_ref[...], wupb_ref[:, n:n + 512])
        merged = (jax.nn.sigmoid(z_ref[:, ga0 + n:ga0 + n + 512]) * ua
                  + jax.nn.sigmoid(z_ref[:, gb0 + n:gb0 + n + 512]) * ub)
        m_ref[:, n:n + 512] = merged.astype(BF16)
    x1 = x_ref[0] + _dot(m_ref[...], wout_ref[...])
    x1_ref[0] = x1
    h2 = _rms(x1, gffn_ref[...])
    _store_token_rows(h2_ref, h2)
    hi = h2.astype(BF16)
    lo = (h2 - hi.astype(F32)).astype(BF16)
    l2 = _dot(hi, wrt2_ref[...])
    lg_ref[0] = l2[:, :ROUTE_COLS] + l2[:, ROUTE_COLS:] + _dot(lo, wrt1_ref[...])


def _mixer(x, g_mix, w_in, ln_g, ln_b, w_sp, bfull, w_up_a, lbp, norm_g, w_up_b, w_out, g_ffn,
           w_rt2, w_rt1):
    bsz, seq, d = x.shape
    ts = min(MIX_TILE, seq)
    in_cols = w_in.shape[1]

    def full(a):
        nd = a.ndim
        return pl.BlockSpec(a.shape, lambda b, j, _nd=nd: (0,) * _nd, pipeline_mode=pl.Buffered(1))

    consts = (g_mix, w_in, ln_g, ln_b, w_sp, bfull, w_up_a, lbp, norm_g, w_up_b, w_out, g_ffn,
              w_rt2, w_rt1)
    assert d == ROW_TILE * LANES
    nj = seq // ts
    tile = lambda w: pl.BlockSpec((1, ts, w), lambda b, j: (b, j, 0))
    return pl.pallas_call(
        _mixer_kernel,
        grid=(bsz, nj),
        in_specs=[tile(d)] + [full(a) for a in consts],
        out_specs=[tile(d),
                   pl.BlockSpec((ts * ROW_TILE, LANES), lambda b, j: (b * nj + j, 0)),
                   tile(ROUTE_COLS)],
        out_shape=[jax.ShapeDtypeStruct((bsz, seq, d), F32),
                   jax.ShapeDtypeStruct((bsz * seq * ROW_TILE, LANES), F32),
                   jax.ShapeDtypeStruct((bsz, seq, ROUTE_COLS), F32)],
        scratch_shapes=[pltpu.VMEM((ts, in_cols), F32),
                        pltpu.VMEM((ts, d), BF16),
                        pltpu.VMEM((ts, GM_WIDTH), BF16),
                        pltpu.VMEM((ts, HG_KW), BF16),
                        pltpu.VMEM((ts, d), BF16),
                        pltpu.VMEM((HG_HEADS, HG_KEY, HG_KEY), F32),
                        pltpu.VMEM((ts, HG_KW), BF16),
                        pltpu.VMEM((ts, HG_KW), F32),
                        pltpu.VMEM((ts // HG_CHUNK, HG_HEADS, HG_KEY, HG_KEY), F32),
                        pltpu.VMEM((ts // HG_CHUNK, HG_HEADS, HG_KEY, HG_KEY), BF16),
                        pltpu.VMEM((ts // HG_CHUNK, HG_KW), F32),
                        pltpu.VMEM((ts, HG_KW), F32),
                        pltpu.VMEM((ts, HG_KW), BF16),
                        pltpu.VMEM((ts, HG_KW), BF16),
                        pltpu.VMEM((ts // (2 * HG_CHUNK), HG_HEADS, 2 * HG_CHUNK, 2 * HG_CHUNK),
                                   BF16)],
        compiler_params=pltpu.CompilerParams(
            dimension_semantics=("arbitrary", "arbitrary"), vmem_limit_bytes=VMEM_LIMIT),
        name="mixer",
    )(x, *consts)


def _route_kernel(lg_ref, utri_ref, dest_ref, gt_ref, binfo_ref, einfo_ref, rec_ref, cnt_ref):
    phase = pl.program_id(0)
    i = pl.program_id(1)
    r = utri_ref.shape[0]
    n_sub = lg_ref.shape[0] // r
    sub8 = lax.broadcasted_iota(jnp.int32, (8, r), 0)
    e32 = lax.broadcasted_iota(jnp.int32, (N_EXPERTS, r), 0).astype(F32)

    @pl.when(jnp.logical_and(phase == 0, i == 0))
    def _():
        cnt_ref[...] = jnp.zeros_like(cnt_ref)

    def rank_tile(s):
        lt = lg_ref[s * r:(s + 1) * r, :].T
        l0, l1, l2, l3 = lt[0:1], lt[1:2], lt[2:3], lt[3:4]
        gmax = jnp.maximum(jnp.maximum(l0, l1), jnp.maximum(l2, l3))
        gsum = (jnp.exp(l0 - gmax) + jnp.exp(l1 - gmax)) + (jnp.exp(l2 - gmax) + jnp.exp(l3 - gmax))
        p_g = 1.0 / gsum
        gsel = jnp.where(l0 == gmax, 0, jnp.where(l1 == gmax, 1, jnp.where(l2 == gmax, 2, 3)))
        eg = [lt[EXP_ROW0 + EXP_PER_GROUP * g:EXP_ROW0 + EXP_PER_GROUP * (g + 1)]
              for g in range(N_GROUPS)]
        sel = jnp.where(gsel == 0, eg[0], jnp.where(gsel == 1, eg[1],
                                                     jnp.where(gsel == 2, eg[2], eg[3])))
        m1 = jnp.max(sel, axis=0, keepdims=True)
        i1 = jnp.min(jnp.where(sel == m1, sub8, EXP_PER_GROUP), axis=0, keepdims=True)
        sel2 = jnp.where(sub8 == i1, -jnp.inf, sel)
        m2 = jnp.max(sel2, axis=0, keepdims=True)
        i2 = jnp.min(jnp.where(sel2 == m2, sub8, EXP_PER_GROUP), axis=0, keepdims=True)
        ex = jnp.exp(m2 - m1)
        g1 = 1.0 / (1.0 + ex)
        g2 = ex * g1
        eid1 = (gsel * EXP_PER_GROUP + i1).astype(F32)
        eid2 = (gsel * EXP_PER_GROUP + i2).astype(F32)
        oh1 = jnp.where(e32 == eid1, 1.0, 0.0)
        oh2 = jnp.where(e32 == eid2, 1.0, 0.0)
        cum1 = _dot(oh1.astype(BF16), utri_ref[...])
        cum2 = _dot(oh2.astype(BF16), utri_ref[...])
        tot1 = jnp.sum(oh1, axis=1, keepdims=True)
        tot2 = jnp.sum(oh2, axis=1, keepdims=True)
        base = cnt_ref[:, 0:1]
        rank1 = jnp.sum(oh1 * (base + cum1), axis=0, keepdims=True)
        rank2 = jnp.sum(oh2 * (base + tot1 + cum2), axis=0, keepdims=True)
        cnt_ref[...] = cnt_ref[...] + (tot1 + tot2)
        rec = jnp.where(sub8 == 0, eid1, jnp.where(sub8 == 1, eid2, jnp.where(
            sub8 == 2, rank1, jnp.where(sub8 == 3, rank2, jnp.where(
                sub8 == 4, g1 * p_g, jnp.where(sub8 == 5, g2 * p_g, 0.0))))))
        rec_ref[i * n_sub + s] = rec

    @pl.when(phase == 0)
    def _():
        for s in range(n_sub):
            rank_tile(s)

    @pl.when(phase == 1)
    def _():
        cnt = cnt_ref[...]
        padded = jnp.floor((cnt + (FFN_BLOCK - 1)) * (1.0 / FFN_BLOCK)) * FFN_BLOCK
        rows = lax.broadcasted_iota(jnp.int32, cnt.shape, 0)
        pad_end = padded
        for s in (1, 2, 4, 8, 16):
            pad_end = pad_end + jnp.where(rows >= s, pltpu.roll(pad_end, s, axis=0), 0.0)
        pad_start = (pad_end - padded)[:, 0:1]
        subw = lax.broadcasted_iota(jnp.int32, (ROUTE_COLS, r), 0)
        for s in range(n_sub):
            rec = rec_ref[i * n_sub + s]
            d1 = rec[2:3] + jnp.sum(jnp.where(e32 == rec[0:1], pad_start, 0.0), axis=0,
                                    keepdims=True)
            d2 = rec[3:4] + jnp.sum(jnp.where(e32 == rec[1:2], pad_start, 0.0), axis=0,
                                    keepdims=True)
            dest_ref[s] = jnp.where(sub8 == 0, d1, jnp.where(sub8 == 1, d2, 0.0)).astype(jnp.int32)
            gates = jnp.where(subw == 0, rec[4:5], jnp.where(subw == 1, rec[5:6], 0.0))
            gt_ref[s * r:(s + 1) * r, :] = gates.T
        nbp = binfo_ref.shape[1]
        blk_start = (lax.broadcasted_iota(jnp.int32, (N_EXPERTS, nbp), 1) * FFN_BLOCK).astype(F32)
        n_le = jnp.sum(jnp.where(pad_end[:, 0:1] <= blk_start, 1.0, 0.0), axis=0, keepdims=True)
        binfo_ref[...] = jnp.broadcast_to(n_le, binfo_ref.shape).astype(jnp.int32)
        lane = lax.broadcasted_iota(jnp.int32, cnt.shape, 1)
        einfo_ref[...] = jnp.where(lane == 0, pad_end - padded + cnt, pad_end).astype(jnp.int32)


def _route(logits, nbp):
    t = logits.shape[0]
    r = min(ROUTE_TILE, t)
    rr = min(ROUTE_SUBTILES * r, t)
    nt = t // rr
    utri = jnp.triu(jnp.ones((r, r), F32), k=1).astype(BF16)
    return pl.pallas_call(
        _route_kernel,
        grid=(2, nt),
        in_specs=[pl.BlockSpec((rr, ROUTE_COLS), lambda p, i: (i * (1 - p) + (nt - 1) * p, 0)),
                  pl.BlockSpec((r, r), lambda p, i: (0, 0))],
        out_specs=[pl.BlockSpec((rr // r, 8, r), lambda p, i: (i * p, 0, 0)),
                   pl.BlockSpec((rr, ROUTE_COLS), lambda p, i: (i * p, 0)),
                   pl.BlockSpec((8, nbp), lambda p, i: (0, 0)),
                   pl.BlockSpec((N_EXPERTS, LANES), lambda p, i: (0, 0))],
        out_shape=[jax.ShapeDtypeStruct((t // r, 8, r), jnp.int32),
                   jax.ShapeDtypeStruct((t, ROUTE_COLS), F32),
                   jax.ShapeDtypeStruct((8, nbp), jnp.int32),
                   jax.ShapeDtypeStruct((N_EXPERTS, LANES), jnp.int32)],
        scratch_shapes=[pltpu.VMEM((t // r, 8, r), F32),
                        pltpu.VMEM((N_EXPERTS, LANES), F32)],
        compiler_params=pltpu.CompilerParams(
            dimension_semantics=("arbitrary", "arbitrary"), vmem_limit_bytes=VMEM_LIMIT),
        name="route",
    )(logits, utri)


DMA_UNROLL = 8


def _token_copy(src, src_tok, dst, dst_tok, sem):
    return pltpu.make_async_copy(
        src.at[pl.ds(pl.multiple_of(src_tok * ROW_TILE, ROW_TILE), ROW_TILE)],
        dst.at[pl.ds(pl.multiple_of(dst_tok * ROW_TILE, ROW_TILE), ROW_TILE)], sem)


def _pad_copies(plo_ref, phi_ref, zero_ref, buf_hbm, sem):
    copies = []
    for e in range(N_EXPERTS):
        lo = plo_ref[e]
        n = phi_ref[e] - lo
        bit = FFN_BLOCK // 2
        while bit >= 1:
            first = lo + (n & ~(2 * bit - 1))
            copies.append(((n & bit) != 0, pltpu.make_async_copy(
                zero_ref.at[pl.ds(0, bit * ROW_TILE)],
                buf_hbm.at[pl.ds(pl.multiple_of(first * ROW_TILE, ROW_TILE), bit * ROW_TILE)],
                sem)))
            bit //= 2
    blk_rows = FFN_BLOCK * ROW_TILE
    n_blocks = buf_hbm.shape[0] // blk_rows
    n_used = phi_ref[N_EXPERTS - 1] // FFN_BLOCK
    for j in range(N_EXPERTS):
        for half in range(2):
            row0 = pl.multiple_of((n_used + j) * blk_rows + half * (blk_rows // 2), ROW_TILE)
            copies.append((n_used + j < n_blocks, pltpu.make_async_copy(
                zero_ref, buf_hbm.at[pl.ds(row0, blk_rows // 2)], sem)))
    return copies


def _scatter_kernel(plo_ref, phi_ref, dest_ref, h2_ref, buf_hbm, zero_ref, sem, zsem):
    r = dest_ref.shape[0] // 2

    @pl.when(pl.program_id(0) == 0)
    def _():
        zero_ref[...] = jnp.zeros_like(zero_ref)
        for cond, cp in _pad_copies(plo_ref, phi_ref, zero_ref, buf_hbm, zsem):
            pl.when(cond)(cp.start)
        for cond, cp in _pad_copies(plo_ref, phi_ref, zero_ref, buf_hbm, zsem):
            pl.when(cond)(cp.wait)

    def issue(g, carry):
        t0 = g * DMA_UNROLL
        idx = [[dest_ref[k * r + t0 + u] for k in range(2)] for u in range(DMA_UNROLL)]
        for u in range(DMA_UNROLL):
            for k in range(2):
                _token_copy(h2_ref, t0 + u, buf_hbm, idx[u][k], sem).start(priority=k)
        return carry

    lax.fori_loop(0, r // DMA_UNROLL, issue, 0)

    def drain(t, carry):
        _token_copy(h2_ref, 0, buf_hbm, 0, sem).wait()
        return carry

    lax.fori_loop(0, 2 * r, drain, 0, unroll=DMA_UNROLL)


def _scatter(pad_lo, pad_hi, dest, r, h2, n_rows):
    nt = dest.shape[0] // (2 * r)
    return pl.pallas_call(
        _scatter_kernel,
        grid_spec=pltpu.PrefetchScalarGridSpec(
            num_scalar_prefetch=2,
            grid=(nt,),
            in_specs=[pl.BlockSpec((2 * r,), lambda i, lo, hi: (i,), memory_space=pltpu.SMEM),
                      pl.BlockSpec((r * ROW_TILE, LANES), lambda i, lo, hi: (i, 0))],
            out_specs=pl.BlockSpec(memory_space=pl.ANY),
            scratch_shapes=[pltpu.VMEM((FFN_BLOCK // 2 * ROW_TILE, LANES), F32),
                            pltpu.SemaphoreType.DMA(()),
                            pltpu.SemaphoreType.DMA(())]),
        out_shape=jax.ShapeDtypeStruct((n_rows * ROW_TILE, LANES), F32),
        compiler_params=pltpu.CompilerParams(
            dimension_semantics=("arbitrary",), has_side_effects=True),
        name="scatter",
    )(pad_lo, pad_hi, dest, h2)


FFN_SLOTS = 3


def _ffn_kernel(be_ref, nu_ref, ne_ref, run_ref, x_hbm, w1_hbm, w3_hbm, w2_hbm, y_ref,
                w13_s, w2_s, x_s, w1_f, w3_f, w2_f, xsems, wsems):
    i = pl.program_id(0)
    blk_rows = FFN_BLOCK * ROW_TILE

    def weight_copies(expert, slot):
        return [pltpu.make_async_copy(src.at[expert], dst.at[slot], wsems.at[slot])
                for src, dst in ((w1_hbm, w1_f), (w3_hbm, w3_f), (w2_hbm, w2_f))]

    def block_copy(blk):
        src = jnp.minimum(blk, nu_ref[0] - 1)
        return pltpu.make_async_copy(
            x_hbm.at[pl.ds(pl.multiple_of(src * blk_rows, blk_rows), blk_rows)],
            x_s.at[blk % FFN_SLOTS], xsems.at[blk % FFN_SLOTS])

    @pl.when(i == 0)
    def _():
        for cp in weight_copies(be_ref[0], 0):
            cp.start()
        block_copy(0).start()
        block_copy(1).start()

    block_copy(i + 2).start()
    block_copy(i).wait()
    x_ref = x_s.at[i % FFN_SLOTS]

    @pl.when(i + 1 == pl.num_programs(0))
    def _():
        block_copy(i + 1).wait()
        block_copy(i + 2).wait()

    e = be_ref[i]
    used = e < N_EXPERTS
    fresh = jnp.logical_or(i == 0, e != be_ref[jnp.maximum(i - 1, 0)])

    half = D_EXPERT // 2
    wslot = run_ref[i] % 2

    @pl.when(jnp.logical_and(used, fresh))
    def _():
        for cp in weight_copies(e, wslot):
            cp.wait()
        for c in range(2):
            w13_s[c, :, :half] = w1_f[wslot, :, c * half:(c + 1) * half].astype(BF16)
            w13_s[c, :, half:] = w3_f[wslot, :, c * half:(c + 1) * half].astype(BF16)
        w2_s[...] = w2_f[wslot].astype(BF16)
        nxt = ne_ref[i]

        @pl.when(nxt < N_EXPERTS)
        def _():
            for cp in weight_copies(nxt, 1 - wslot):
                cp.start()

    @pl.when(used)
    def _():
        kw = 2 * LANES
        h = [None, None]
        for kc in range(w13_s.shape[1] // kw):
            xk = jnp.concatenate(
                [x_ref[pl.ds(2 * kc + j, FFN_BLOCK, stride=ROW_TILE), :] for j in range(2)],
                axis=1).astype(BF16)
            for c in range(2):
                part = _dot(xk, w13_s[c, kc * kw:(kc + 1) * kw, :])
                h[c] = part if h[c] is None else h[c] + part
        acts = []
        for c in range(2):
            a = h[c][:, :half]
            acts.append((a * jax.nn.sigmoid(a) * h[c][:, half:]).astype(BF16))
        for nc in range(y_ref.shape[0] // FFN_BLOCK // 2):
            cols = slice(nc * kw, (nc + 1) * kw)
            y = (_dot(acts[0], w2_s[0:half, cols]) + _dot(acts[1], w2_s[half:2 * half, cols]))
            for j in range(2):
                y_ref[pl.ds(2 * nc + j, FFN_BLOCK, stride=ROW_TILE), :] = (
                    y[:, j * LANES:(j + 1) * LANES])

    @pl.when(jnp.logical_not(used))
    def _():
        y_ref[...] = jnp.zeros_like(y_ref)


def _ffn(blk_e, n_used, buf, w1, w3, w2):
    nb = buf.shape[0] // (FFN_BLOCK * ROW_TILE)
    d = w1.shape[1]
    pos = jnp.arange(nb, dtype=jnp.int32)
    change = jnp.concatenate(
        [jnp.zeros((1,), jnp.int32), (blk_e[1:] != blk_e[:-1]).astype(jnp.int32)])
    run = jnp.sum(jnp.where(pos[None, :] <= pos[:, None], change[None, :], 0), axis=1)
    later = jnp.where(blk_e[None, :] > blk_e[:, None], blk_e[None, :], N_EXPERTS)
    next_e = jnp.min(later, axis=1).astype(jnp.int32)
    rows = pl.BlockSpec((FFN_BLOCK * ROW_TILE, LANES), lambda i, be, nu, ne, rn: (i, 0))
    hbm = pl.BlockSpec(memory_space=pl.ANY)
    return pl.pallas_call(
        _ffn_kernel,
        grid_spec=pltpu.PrefetchScalarGridSpec(
            num_scalar_prefetch=4,
            grid=(nb,),
            in_specs=[hbm, hbm, hbm, hbm],
            out_specs=rows,
            scratch_shapes=[pltpu.VMEM((2, d, D_EXPERT), BF16),
                            pltpu.VMEM((D_EXPERT, d), BF16),
                            pltpu.VMEM((FFN_SLOTS, FFN_BLOCK * ROW_TILE, LANES), F32),
                            pltpu.VMEM((2, d, D_EXPERT), F32),
                            pltpu.VMEM((2, d, D_EXPERT), F32),
                            pltpu.VMEM((2, D_EXPERT, d), F32),
                            pltpu.SemaphoreType.DMA((FFN_SLOTS,)),
                            pltpu.SemaphoreType.DMA((2,))]),
        out_shape=jax.ShapeDtypeStruct(buf.shape, F32),
        compiler_params=pltpu.CompilerParams(
            dimension_semantics=("arbitrary",), vmem_limit_bytes=VMEM_LIMIT),
        name="ffn",
    )(blk_e, n_used, next_e, run, buf, w1, w3, w2)


GATHER_SLOTS = 3


def _final_kernel(dest_ref, dn1_ref, dn2_ref, x1_ref, p_ref, gt_ref, ybuf_hbm, gple_ref, wpg_ref,
                  wple_ref, gfin_ref, out_ref, y_ref, x2_ref, h3_ref, g_ref, pb_ref, sems):
    i = pl.program_id(0)
    r = x1_ref.shape[0]
    slot = i % GATHER_SLOTS
    ahead = (i + GATHER_SLOTS - 1) % GATHER_SLOTS

    def issue_group(d_ref, s, t0):
        idx = [[d_ref[k * r + t0 + u] for k in range(2)] for u in range(DMA_UNROLL)]
        for u in range(DMA_UNROLL):
            for k in range(2):
                _token_copy(ybuf_hbm, idx[u][k], y_ref.at[s, k], t0 + u,
                            sems.at[s]).start(priority=k)

    def drain(s):
        def body(t, carry):
            _token_copy(ybuf_hbm, 0, y_ref.at[s, 0], 0, sems.at[s]).wait()
            return carry

        lax.fori_loop(0, 2 * r, body, 0, unroll=DMA_UNROLL)

    @pl.when(i == 0)
    def _():
        def body(g, carry):
            issue_group(dest_ref, 0, g * DMA_UNROLL)
            issue_group(dn1_ref, 1, g * DMA_UNROLL)
            return carry

        lax.fori_loop(0, r // DMA_UNROLL, body, 0)

    drain(slot)

    n_batch = 4
    per = r // n_batch

    def issue_batch(b):
        for t0 in range(b * per, (b + 1) * per, DMA_UNROLL):
            issue_group(dn2_ref, ahead, t0)

    n_chunk = 8
    rc = r // n_chunk
    issue_batch(0)
    for c in range(n_chunk):
        rows = pl.ds(c * rc, rc)
        x2 = (x1_ref[rows, :]
              + gt_ref[rows, 0:1] * _load_token_rows(y_ref.at[slot, 0], rc, c * rc)
              + gt_ref[rows, 1:2] * _load_token_rows(y_ref.at[slot, 1], rc, c * rc))
        x2_ref[rows, :] = x2
        h3_ref[rows, :] = _rms(x2, gple_ref[...]).astype(BF16)
    issue_batch(1)
    pb_ref[...] = p_ref[...].astype(BF16)
    for n in range(0, x2_ref.shape[1], 256):
        cols = slice(n, n + 256)
        g_ref[:, cols] = (jax.nn.sigmoid(_dot(h3_ref[...], wpg_ref[:, cols]))
                          * _dot(pb_ref[...], wple_ref[:, cols]))
        if n == 256:
            issue_batch(2)
    issue_batch(3)
    for c in range(n_chunk):
        rows = pl.ds(c * rc, rc)
        out_ref[rows, :] = _rms(x2_ref[rows, :] + g_ref[rows, :], gfin_ref[...])

    @pl.when(i + 1 == pl.num_programs(0))
    def _():
        for other in range(1, GATHER_SLOTS):
            drain((i + other) % GATHER_SLOTS)


def _final(dest, r, x1, p, gt, ybuf, g_ple, w_pg, w_ple, g_final):
    t, d = x1.shape
    nt = t // r

    def full(a):
        nd = a.ndim
        return pl.BlockSpec(a.shape, lambda i, _nd=nd: (0,) * _nd, pipeline_mode=pl.Buffered(1))

    return pl.pallas_call(
        _final_kernel,
        grid=(nt,),
        in_specs=[pl.BlockSpec((2 * r,), lambda i: (i,), memory_space=pltpu.SMEM),
                  pl.BlockSpec((2 * r,), lambda i: (jnp.minimum(i + 1, nt - 1),),
                               memory_space=pltpu.SMEM),
                  pl.BlockSpec((2 * r,), lambda i: (jnp.minimum(i + 2, nt - 1),),
                               memory_space=pltpu.SMEM),
                  pl.BlockSpec((r, d), lambda i: (i, 0)),
                  pl.BlockSpec((r, p.shape[1]), lambda i: (i, 0)),
                  pl.BlockSpec((r, ROUTE_COLS), lambda i: (i, 0)),
                  pl.BlockSpec(memory_space=pl.ANY),
                  full(g_ple), full(w_pg), full(w_ple), full(g_final)],
        out_specs=pl.BlockSpec((r, d), lambda i: (i, 0)),
        out_shape=jax.ShapeDtypeStruct((t, d), F32),
        scratch_shapes=[pltpu.VMEM((GATHER_SLOTS, 2, r * ROW_TILE, LANES), F32),
                        pltpu.VMEM((r, d), F32),
                        pltpu.VMEM((r, d), BF16),
                        pltpu.VMEM((r, d), F32),
                        pltpu.VMEM((r, p.shape[1]), BF16),
                        pltpu.SemaphoreType.DMA((GATHER_SLOTS,))],
        compiler_params=pltpu.CompilerParams(
            dimension_semantics=("arbitrary",), vmem_limit_bytes=VMEM_LIMIT),
        name="final",
    )(dest, dest, dest, x1, p, gt, ybuf, g_ple, w_pg, w_ple, g_final)


def kernel(x, p, g_mix, w_in, gm_ln_g, gm_ln_b, gm_w_sp, gm_b_sp, w_up_a, hg_lb_param, hg_norm_g,
           w_up_b, w_out, g_ffn, w_grp, w_exp, w1, w3, w2, g_ple, w_pg, w_ple, g_final):
    bsz, seq, d = x.shape
    t = bsz * seq
    row = lambda a: a.reshape(1, -1)

    assert w_in.shape[0] == 1 and hg_lb_param.shape[0] == 2, "single-layer block"
    i = 0
    w_rt = jnp.zeros((d, ROUTE_COLS), F32)
    w_rt = w_rt.at[:, 0:N_GROUPS].set(w_grp[i])
    w_rt = w_rt.at[:, EXP_ROW0:EXP_ROW0 + N_EXPERTS].set(w_exp[i])
    w_rt_hi = w_rt.astype(BF16)
    w_rt_lo = (w_rt - w_rt_hi.astype(F32)).astype(BF16)
    w_rt2 = jnp.concatenate([w_rt_hi, w_rt_lo], axis=1)
    bfull = jnp.repeat(gm_b_sp[i].T, GM_WIDTH // GM_GROUPS, axis=1)

    x1, h2, logits = _mixer(
        x, row(g_mix[i]), w_in[i].astype(BF16), row(gm_ln_g[i]), row(gm_ln_b[i]), gm_w_sp[i],
        bfull, w_up_a[i].astype(BF16), hg_lb_param, row(hg_norm_g[i]),
        w_up_b[i].astype(BF16), w_out[i].astype(BF16), row(g_ffn[i]), w_rt2, w_rt_hi)

    n_rows = 2 * t + N_EXPERTS * FFN_BLOCK
    nb = n_rows // FFN_BLOCK
    nbp = -(-nb // LANES) * LANES
    dest, gt, binfo, einfo = _route(logits.reshape(t, ROUTE_COLS), nbp)
    r = dest.shape[2]
    dest = dest[:, 0:2, :].reshape(-1)
    buf = _scatter(einfo[:, 0], einfo[:, 1], dest, r, h2, n_rows)
    n_used = einfo[N_EXPERTS - 1:, 1] // FFN_BLOCK
    ybuf = _ffn(binfo[0, :nb], n_used, buf, w1[i], w3[i], w2[i])
    out = _final(dest, r, x1.reshape(t, d), p[i].reshape(t, -1), gt, ybuf, row(g_ple[i]),
                 w_pg[i].astype(BF16), w_ple[i].astype(BF16), row(g_final))
    return out.reshape(bsz, seq, d)
```
